```python
import jax, jax.numpy as jnp
from jax import lax
import numpy as np

D_MODEL = 2048
BATCH = 2
SEQ = 4096
DEPTH = 2

HEAD_DIM = 128
Q_BLOCK = 128
NEG = -1e30
FORCE = 1e30
EPS = 1e-5

A_HEADS = 8
A_KV_RANK = 256
IDX_HEADS = 16
IDX_DIM = 64
DSA_TOPK_MAX = 256

B_HEADS = 8
B_KV_GROUPS = 2
NSA_CMP_LEN = 32
NSA_CMP_STRIDE = 16
NSA_SLC_LEN = 64
NSA_SLC_N = 16
NSA_WINDOW = 512

CONV_CH = 1024
CONV_W = 3

SSD_HEAD_DIM = 64
SSD_HEADS = 48
SSD_INNER = SSD_HEADS * SSD_HEAD_DIM
SSD_GROUPS = 8
SSD_STATE = 128
SSD_CONV_W = 4
SSD_CHUNK = 128

PEER_HEADS = 8
PEER_NKEYS = 128
PEER_N = PEER_NKEYS * PEER_NKEYS
PEER_DKEY = 256
PEER_TOPK_HALF = 16
PEER_TOPK = 16

ALPHA = (2 * DEPTH) ** 0.25
BETA = (8 * DEPTH) ** -0.25

SPLITS0 = (A_HEADS * HEAD_DIM, A_KV_RANK, IDX_HEADS * IDX_DIM, IDX_DIM, IDX_HEADS, B_HEADS * HEAD_DIM) + (B_KV_GROUPS * HEAD_DIM,) * 6 + (B_HEADS * 3,)
IN0 = sum(SPLITS0)
MIX0 = A_HEADS * HEAD_DIM + B_HEADS * HEAD_DIM
SPLITS1 = (CONV_CH,) * 3 + (SSD_INNER, SSD_INNER, SSD_GROUPS * SSD_STATE, SSD_GROUPS * SSD_STATE, SSD_HEADS)
IN1 = sum(SPLITS1)
MIX1 = CONV_CH + SSD_INNER

kernel_name = 'hybrid_dsa_nsa_shortconv_ssd_peer_deepnorm'


def _split(y, sizes):
    cuts = [int(c) for c in np.cumsum(sizes)[:-1]]
    return jnp.split(y, cuts, axis=-1)


def _rows(a, start):
    return lax.dynamic_slice_in_dim(a, start, Q_BLOCK, axis=1)


def layer_norm(x, g, b):
    xf = x.astype(jnp.float32)
    mu = jnp.mean(xf, axis=-1, keepdims=True)
    var = jnp.mean(jnp.square(xf - mu), axis=-1, keepdims=True)
    return ((xf - mu) * lax.rsqrt(var + EPS)).astype(x.dtype) * g + b


def rms_norm(x, g):
    xf = x.astype(jnp.float32)
    return (xf * lax.rsqrt(jnp.mean(xf * xf, axis=-1, keepdims=True) + EPS)).astype(x.dtype) * g


def causal_depthwise_conv(u, w):
    width, ch = w.shape
    return lax.conv_general_dilated(u, w[:, None, :], window_strides=(1,), padding=[(width - 1, 0)],
                                    dimension_numbers=('NWC', 'WIO', 'NWC'), feature_group_count=ch)


def dsa_attention(q, c_kv, iq, ik, iw, kv_gain, w_uk, w_uv):
    bsz, seq = q.shape[0], q.shape[1]
    n_keys = seq
    topk = min(DSA_TOPK_MAX, n_keys // 4)
    c_kv = rms_norm(c_kv, kv_gain)
    q_lat = jnp.einsum('bshd,hrd->bshr', q, w_uk)
    key_pos = jnp.arange(seq)
    gather_rows = jax.vmap(lambda t, ix: t[ix])
    idx_scale = (IDX_DIM * IDX_HEADS) ** -0.5
    att_scale = HEAD_DIM ** -0.5

    def block(i):
        start = i * Q_BLOCK
        qpos = start + jnp.arange(Q_BLOCK)
        q_i, iq_i, iw_i = _rows(q_lat, start), _rows(iq, start), _rows(iw, start)
        dots = jax.nn.relu(jnp.einsum('bqhd,bsd->bqsh', iq_i, ik))
        score = jnp.einsum('bqsh,bqh->bqs', dots, iw_i).astype(jnp.float32) * idx_scale
        causal = key_pos[None, :] <= qpos[:, None]
        score = jnp.where(causal[None], score, NEG)
        _, sel = lax.top_k(score, topk)
        valid = sel <= qpos[None, :, None]
        c_sel = gather_rows(c_kv, sel)
        s = jnp.einsum('bqhr,bqkr->bqhk', q_i, c_sel).astype(jnp.float32) * att_scale
        s = jnp.where(valid[:, :, None, :], s, NEG)
        p = jax.nn.softmax(s, axis=-1).astype(c_sel.dtype)
        return jnp.einsum('bqhk,bqkr->bqhr', p, c_sel)

    o_lat = lax.map(block, jnp.arange(seq // Q_BLOCK))
    o_lat = jnp.moveaxis(o_lat, 0, 1).reshape(bsz, seq, A_HEADS, A_KV_RANK)
    return jnp.einsum('bshr,hrd->bshd', o_lat, w_uv)


def nsa_compress(k, pos, w1, w2):
    seq = k.shape[1]
    nc = (seq - NSA_CMP_LEN) // NSA_CMP_STRIDE + 1
    idx = jnp.arange(nc)[:, None] * NSA_CMP_STRIDE + jnp.arange(NSA_CMP_LEN)[None, :]
    blocks = k[:, idx] + pos[None, None, :, None, :]
    h = jax.nn.gelu(jnp.einsum('bclgd,lde->bcge', blocks, w1))
    return jnp.einsum('bcge,ef->bcgf', h, w2)


def nsa_attention(q, k_c, v_c, k_s, v_s, k_w, v_w, gates,
                  cmpk_pos, cmpk_w1, cmpk_w2, cmpv_pos, cmpv_w1, cmpv_w2):
    bsz, seq = q.shape[0], q.shape[1]
    hpg = B_HEADS // B_KV_GROUPS
    scale = HEAD_DIM ** -0.5
    kc = nsa_compress(k_c, cmpk_pos, cmpk_w1, cmpk_w2)
    vc = nsa_compress(v_c, cmpv_pos, cmpv_w1, cmpv_w2)
    nc = kc.shape[1]
    cmp_start = jnp.arange(nc) * NSA_CMP_STRIDE
    cmp_end = cmp_start + NSA_CMP_LEN - 1
    ns = seq // NSA_SLC_LEN
    nsel = min(NSA_SLC_N, ns)
    slc_start = jnp.arange(ns) * NSA_SLC_LEN
    overlap = ((cmp_start[:, None] < slc_start[None, :] + NSA_SLC_LEN) &
               (cmp_start[:, None] + NSA_CMP_LEN > slc_start[None, :])).astype(jnp.float32)
    blk_id = jnp.arange(ns)
    ks_blocks = k_s.reshape(bsz, ns, NSA_SLC_LEN, B_KV_GROUPS, HEAD_DIM).transpose(0, 3, 1, 2, 4)
    vs_blocks = v_s.reshape(bsz, ns, NSA_SLC_LEN, B_KV_GROUPS, HEAD_DIM).transpose(0, 3, 1, 2, 4)
    kw_pad = jnp.pad(k_w, ((0, 0), (NSA_WINDOW, 0), (0, 0), (0, 0)))
    vw_pad = jnp.pad(v_w, ((0, 0), (NSA_WINDOW, 0), (0, 0), (0, 0)))
    gate = jax.nn.sigmoid(gates.astype(jnp.float32)).astype(q.dtype)
    gather_blocks = jax.vmap(jax.vmap(lambda t, ix: t[ix]))

    def block(i):
        start = i * Q_BLOCK
        qpos = start + jnp.arange(Q_BLOCK)
        q_i = _rows(q, start).reshape(bsz, Q_BLOCK, B_KV_GROUPS, hpg, HEAD_DIM)
        g_i = _rows(gate, start).reshape(bsz, Q_BLOCK, B_KV_GROUPS, hpg, 3)
        vis_c = cmp_end[None, :] <= qpos[:, None]
        s_c = jnp.einsum('bqghd,bcgd->bqghc', q_i, kc).astype(jnp.float32) * scale
        s_c = jnp.where(vis_c[None, :, None, None, :], s_c, NEG)
        p_c = jax.nn.softmax(s_c, axis=-1) * vis_c.any(-1)[None, :, None, None, None]
        o_c = jnp.einsum('bqghc,bcgd->bqghd', p_c.astype(vc.dtype), vc)
        imp = jnp.einsum('bqghc,cj->bqgj', p_c, overlap)
        cur = qpos // NSA_SLC_LEN
        forced = (blk_id[None, :] == 0) | (blk_id[None, :] == cur[:, None]) | (blk_id[None, :] == cur[:, None] - 1)
        admissible = blk_id[None, :] <= cur[:, None]
        imp = jnp.where(forced[None, :, None, :], FORCE, imp)
        imp = jnp.where(admissible[None, :, None, :], imp, NEG)
        _, sel = lax.top_k(imp, nsel)
        sel_t = sel.transpose(0, 2, 1, 3)
        ks = gather_blocks(ks_blocks, sel_t).reshape(bsz, B_KV_GROUPS, Q_BLOCK, nsel * NSA_SLC_LEN, HEAD_DIM)
        vs = gather_blocks(vs_blocks, sel_t).reshape(bsz, B_KV_GROUPS, Q_BLOCK, nsel * NSA_SLC_LEN, HEAD_DIM)
        kpos = (sel_t[..., None] * NSA_SLC_LEN + jnp.arange(NSA_SLC_LEN)).reshape(bsz, B_KV_GROUPS, Q_BLOCK, nsel * NSA_SLC_LEN)
        vis_s = (kpos <= qpos[None, None, :, None]).transpose(0, 2, 1, 3)
        s_s = jnp.einsum('bqghd,bgqkd->bqghk', q_i, ks).astype(jnp.float32) * scale
        s_s = jnp.where(vis_s[:, :, :, None, :], s_s, NEG)
        p_s = jax.nn.softmax(s_s, axis=-1).astype(vs.dtype)
        o_s = jnp.einsum('bqghk,bgqkd->bqghd', p_s, vs)
        kw_i = lax.dynamic_slice_in_dim(kw_pad, start, NSA_WINDOW + Q_BLOCK, axis=1)
        vw_i = lax.dynamic_slice_in_dim(vw_pad, start, NSA_WINDOW + Q_BLOCK, axis=1)
        kposw = start - NSA_WINDOW + jnp.arange(NSA_WINDOW + Q_BLOCK)
        vis_w = ((kposw[None, :] <= qpos[:, None]) & (kposw[None, :] > qpos[:, None] - NSA_WINDOW)
                 & (kposw[None, :] >= 0))
        s_w = jnp.einsum('bqghd,bkgd->bqghk', q_i, kw_i).astype(jnp.float32) * scale
        s_w = jnp.where(vis_w[None, :, None, None, :], s_w, NEG)
        p_w = jax.nn.softmax(s_w, axis=-1).astype(vw_i.dtype)
        o_w = jnp.einsum('bqghk,bkgd->bqghd', p_w, vw_i)
        o = g_i[..., 0:1] * o_c + g_i[..., 1:2] * o_s + g_i[..., 2:3] * o_w
        return o.reshape(bsz, Q_BLOCK, B_HEADS, HEAD_DIM)

    o = lax.map(block, jnp.arange(seq // Q_BLOCK))
    return jnp.moveaxis(o, 0, 1).reshape(bsz, seq, B_HEADS, HEAD_DIM)


def ssd_chunked(x, dt, a_neg, bm, cm):
    bsz, seq, nh, hd = x.shape
    ng, ns = bm.shape[2], bm.shape[3]
    rep = nh // ng
    nc, lc = seq // SSD_CHUNK, SSD_CHUNK
    f32 = jnp.float32
    xg = x.astype(f32).reshape(bsz, nc, lc, ng, rep, hd)
    dtg = dt.astype(f32).reshape(bsz, nc, lc, ng, rep)
    bc = bm.astype(f32).reshape(bsz, nc, lc, ng, ns)
    cc = cm.astype(f32).reshape(bsz, nc, lc, ng, ns)
    a_cs = jnp.cumsum(dtg * a_neg.astype(f32).reshape(ng, rep), axis=2)
    xdt = xg * dtg[..., None]
    causal = jnp.tril(jnp.ones((lc, lc), bool))
    seg = a_cs[:, :, :, None] - a_cs[:, :, None, :]
    decay = jnp.exp(jnp.where(causal[None, None, :, :, None, None], seg, -jnp.inf))
    cb = jnp.einsum('bcign,bcjgn->bcijg', cc, bc)
    y_diag = jnp.einsum('bcijg,bcijgr,bcjgrp->bcigrp', cb, decay, xdt)
    decay_s = jnp.exp(a_cs[:, :, -1:] - a_cs)
    states = jnp.einsum('bcjgn,bcjgr,bcjgrp->bcgrpn', bc, decay_s, xdt)
    chunk_decay = jnp.exp(a_cs[:, :, -1])

    def step(h, inp):
        st, dec = inp
        return h * dec[..., None, None] + st, h

    h0 = jnp.zeros((bsz, ng, rep, hd, ns), f32)
    _, prev = lax.scan(step, h0, (jnp.moveaxis(states, 1, 0), jnp.moveaxis(chunk_decay, 1, 0)))
    prev = jnp.moveaxis(prev, 0, 1)
    y_off = jnp.einsum('bcign,bcgrpn,bcigr->bcigrp', cc, prev, jnp.exp(a_cs))
    return (y_diag + y_off).reshape(bsz, seq, nh, hd).astype(x.dtype)


def gated_group_rmsnorm(y, z, g):
    bsz, seq, ch = y.shape
    h = (y * jax.nn.silu(z)).astype(jnp.float32).reshape(bsz, seq, SSD_GROUPS, ch // SSD_GROUPS)
    h = h * lax.rsqrt(jnp.mean(h * h, axis=-1, keepdims=True) + EPS)
    return h.reshape(bsz, seq, ch).astype(y.dtype) * g


def peer(x, w_q, sub_keys, u_tab, v_tab):
    bsz, seq, dm = x.shape
    ntok = bsz * seq
    xt = x.reshape(ntok, dm)
    q = (xt @ w_q).reshape(ntok, PEER_HEADS, 2, PEER_DKEY // 2)
    s = jnp.einsum('thpd,pkd->thpk', q, sub_keys).astype(jnp.float32)
    sv, si = lax.top_k(s, PEER_TOPK_HALF)
    cand_s = (sv[:, :, 0, :, None] + sv[:, :, 1, None, :]).reshape(ntok, PEER_HEADS, -1)
    cand_i = (si[:, :, 0, :, None] * PEER_NKEYS + si[:, :, 1, None, :]).reshape(ntok, PEER_HEADS, -1)
    top_s, top_pos = lax.top_k(cand_s, PEER_TOPK)
    eidx = jnp.take_along_axis(cand_i, top_pos, axis=-1)
    gate = jax.nn.softmax(top_s, axis=-1).astype(x.dtype)
    nb = ntok // Q_BLOCK

    def block(args):
        xb, eb, gb = args
        u = u_tab[eb]
        h = jax.nn.gelu(jnp.einsum('td,thkd->thk', xb, u))
        v = v_tab[eb]
        return jnp.einsum('thk,thkd->td', gb * h, v)

    out = lax.map(block, (xt.reshape(nb, Q_BLOCK, dm), eidx.reshape(nb, Q_BLOCK, PEER_HEADS, PEER_TOPK),
                          gate.reshape(nb, Q_BLOCK, PEER_HEADS, PEER_TOPK)))
    return out.reshape(bsz, seq, dm)


def even_layer(x, w_in, kv_gain, w_uk, w_uv, cmpk_pos, cmpk_w1, cmpk_w2, cmpv_pos, cmpv_w1, cmpv_w2,
               w_out, ln1_g, ln1_b, peer_wq, peer_keys, peer_u, peer_v, ln2_g, ln2_b):
    bsz, seq, _ = x.shape
    y = jnp.einsum('bsd,de->bse', x, w_in)
    (a_q, a_ckv, a_iq, a_ik, a_iw, b_q, b_kc, b_vc, b_ks, b_vs, b_kw, b_vw, b_g) = _split(y, SPLITS0)
    kvs = lambda t: t.reshape(bsz, seq, B_KV_GROUPS, HEAD_DIM)
    o_a = dsa_attention(a_q.reshape(bsz, seq, A_HEADS, HEAD_DIM), a_ckv,
                        a_iq.reshape(bsz, seq, IDX_HEADS, IDX_DIM), a_ik, a_iw, kv_gain, w_uk, w_uv)
    o_b = nsa_attention(b_q.reshape(bsz, seq, B_HEADS, HEAD_DIM), kvs(b_kc), kvs(b_vc), kvs(b_ks), kvs(b_vs),
                        kvs(b_kw), kvs(b_vw), b_g.reshape(bsz, seq, B_HEADS, 3),
                        cmpk_pos, cmpk_w1, cmpk_w2, cmpv_pos, cmpv_w1, cmpv_w2)
    mix = jnp.concatenate([o_a.reshape(bsz, seq, -1), o_b.reshape(bsz, seq, -1)], axis=-1)
    x = layer_norm(ALPHA * x + jnp.einsum('bse,ed->bsd', mix, w_out), ln1_g, ln1_b)
    return layer_norm(ALPHA * x + peer(x, peer_wq, peer_keys, peer_u, peer_v), ln2_g, ln2_b)


def odd_layer(x, w_in, conv_w, ssd_conv_w, ssd_conv_b, dt_bias, a_log, d_skip, norm_g,
              w_out, ln1_g, ln1_b, peer_wq, peer_keys, peer_u, peer_v, ln2_g, ln2_b):
    bsz, seq, _ = x.shape
    y = jnp.einsum('bsd,de->bse', x, w_in)
    c_b, c_c, c_h, z, xs, bs, cs, dt = _split(y, SPLITS1)
    y_c = c_b * causal_depthwise_conv(c_c * c_h, conv_w)
    xbc = jax.nn.silu(causal_depthwise_conv(jnp.concatenate([xs, bs, cs], axis=-1), ssd_conv_w) + ssd_conv_b)
    xs, bs, cs = _split(xbc, (SSD_INNER, SSD_GROUPS * SSD_STATE, SSD_GROUPS * SSD_STATE))
    dt = jax.nn.softplus(dt + dt_bias)
    xh = xs.reshape(bsz, seq, SSD_HEADS, SSD_HEAD_DIM)
    y_d = ssd_chunked(xh, dt, -jnp.exp(a_log), bs.reshape(bsz, seq, SSD_GROUPS, SSD_STATE),
                      cs.reshape(bsz, seq, SSD_GROUPS, SSD_STATE))
    y_d = (y_d + d_skip[:, None] * xh).reshape(bsz, seq, SSD_INNER)
    y_d = gated_group_rmsnorm(y_d, z, norm_g)
    mix = jnp.concatenate([y_c, y_d], axis=-1)
    x = layer_norm(ALPHA * x + jnp.einsum('bse,ed->bsd', mix, w_out), ln1_g, ln1_b)
    return layer_norm(ALPHA * x + peer(x, peer_wq, peer_keys, peer_u, peer_v), ln2_g, ln2_b)


def setup_inputs(seed: int = 0) -> dict:
    key = jax.random.key(seed)
    keys = jax.random.split(key, 64)
    cnt = [0]

    def nk():
        cnt[0] += 1
        return keys[cnt[0] - 1]

    def nrm(shape, scale):
        return jax.random.normal(nk(), shape, jnp.float32) * scale

    def gain(n):
        return 1.0 + nrm((n,), 0.02)

    def bias(n):
        return nrm((n,), 0.01)

    def peer_params(prefix, d):
        d[prefix + 'peer_wq'] = nrm((D_MODEL, PEER_HEADS * PEER_DKEY), D_MODEL ** -0.5)
        d[prefix + 'peer_keys'] = nrm((2, PEER_NKEYS, PEER_DKEY // 2), (PEER_DKEY // 2) ** -0.5)
        d[prefix + 'peer_u'] = nrm((PEER_N, D_MODEL), D_MODEL ** -0.5)
        d[prefix + 'peer_v'] = nrm((PEER_N, D_MODEL), BETA * PEER_HEADS ** -0.5)
        d[prefix + 'ln2_g'] = gain(D_MODEL)
        d[prefix + 'ln2_b'] = bias(D_MODEL)

    d = {}
    d['x'] = nrm((BATCH, SEQ, D_MODEL), 1.0)
    d['l0_w_in'] = nrm((D_MODEL, IN0), D_MODEL ** -0.5)
    d['l0_kv_gain'] = gain(A_KV_RANK)
    d['l0_w_uk'] = nrm((A_HEADS, A_KV_RANK, HEAD_DIM), A_KV_RANK ** -0.5)
    d['l0_w_uv'] = nrm((A_HEADS, A_KV_RANK, HEAD_DIM), A_KV_RANK ** -0.5)
    d['l0_cmpk_pos'] = nrm((NSA_CMP_LEN, HEAD_DIM), 0.02)
    d['l0_cmpk_w1'] = nrm((NSA_CMP_LEN, HEAD_DIM, HEAD_DIM), (NSA_CMP_LEN * HEAD_DIM) ** -0.5)
    d['l0_cmpk_w2'] = nrm((HEAD_DIM, HEAD_DIM), HEAD_DIM ** -0.5)
    d['l0_cmpv_pos'] = nrm((NSA_CMP_LEN, HEAD_DIM), 0.02)
    d['l0_cmpv_w1'] = nrm((NSA_CMP_LEN, HEAD_DIM, HEAD_DIM), (NSA_CMP_LEN * HEAD_DIM) ** -0.5)
    d['l0_cmpv_w2'] = nrm((HEAD_DIM, HEAD_DIM), HEAD_DIM ** -0.5)
    d['l0_w_out'] = nrm((MIX0, D_MODEL), BETA * MIX0 ** -0.5)
    d['l0_ln1_g'] = gain(D_MODEL)
    d['l0_ln1_b'] = bias(D_MODEL)
    peer_params('l0_', d)
    d['l1_w_in'] = nrm((D_MODEL, IN1), D_MODEL ** -0.5)
    d['l1_conv_w'] = nrm((CONV_W, CONV_CH), CONV_W ** -0.5)
    d['l1_ssd_conv_w'] = nrm((SSD_CONV_W, SSD_INNER + 2 * SSD_GROUPS * SSD_STATE), SSD_CONV_W ** -0.5)
    d['l1_ssd_conv_b'] = bias(SSD_INNER + 2 * SSD_GROUPS * SSD_STATE)
    dt0 = jnp.exp(jax.random.uniform(nk(), (SSD_HEADS,), jnp.float32, minval=float(np.log(1e-3)), maxval=float(np.log(1e-1))))
    d['l1_dt_bias'] = dt0 + jnp.log(-jnp.expm1(-dt0))
    d['l1_a_log'] = jnp.log(jax.random.uniform(nk(), (SSD_HEADS,), jnp.float32, minval=1.0, maxval=16.0))
    d['l1_d_skip'] = gain(SSD_HEADS)
    d['l1_norm_g'] = gain(SSD_INNER)
    d['l1_w_out'] = nrm((MIX1, D_MODEL), BETA * MIX1 ** -0.5)
    d['l1_ln1_g'] = gain(D_MODEL)
    d['l1_ln1_b'] = bias(D_MODEL)
    peer_params('l1_', d)
    return d


def reference(x, l0_w_in, l0_kv_gain, l0_w_uk, l0_w_uv, l0_cmpk_pos, l0_cmpk_w1, l0_cmpk_w2,
              l0_cmpv_pos, l0_cmpv_w1, l0_cmpv_w2, l0_w_out, l0_ln1_g, l0_ln1_b,
              l0_peer_wq, l0_peer_keys, l0_peer_u, l0_peer_v, l0_ln2_g, l0_ln2_b,
              l1_w_in, l1_conv_w, l1_ssd_conv_w, l1_ssd_conv_b, l1_dt_bias, l1_a_log, l1_d_skip,
              l1_norm_g, l1_w_out, l1_ln1_g, l1_ln1_b,
              l1_peer_wq, l1_peer_keys, l1_peer_u, l1_peer_v, l1_ln2_g, l1_ln2_b):
    layer_params = [
        (l0_w_in, l0_kv_gain, l0_w_uk, l0_w_uv, l0_cmpk_pos, l0_cmpk_w1, l0_cmpk_w2,
         l0_cmpv_pos, l0_cmpv_w1, l0_cmpv_w2, l0_w_out, l0_ln1_g, l0_ln1_b,
         l0_peer_wq, l0_peer_keys, l0_peer_u, l0_peer_v, l0_ln2_g, l0_ln2_b),
        (l1_w_in, l1_conv_w, l1_ssd_conv_w, l1_ssd_conv_b, l1_dt_bias, l1_a_log, l1_d_skip,
         l1_norm_g, l1_w_out, l1_ln1_g, l1_ln1_b,
         l1_peer_wq, l1_peer_keys, l1_peer_u, l1_peer_v, l1_ln2_g, l1_ln2_b),
    ]
    for i in range(DEPTH):
        if i % 2 == 0:
            x = even_layer(x, *layer_params[i])
        else:
            x = odd_layer(x, *layer_params[i])
    return x
```

```python
import functools

import numpy as np
import jax
import jax.numpy as jnp
from jax import lax
from jax.experimental import pallas as pl
from jax.experimental.pallas import tpu as pltpu

F32 = jnp.float32
BF16 = jnp.bfloat16

D_MODEL = 2048
HEAD_DIM = 128
Q_BLOCK = 128
NEG = -1e30
FORCE = 1e30
EPS = 1e-5

A_HEADS = 8
A_KV_RANK = 256
IDX_HEADS = 16
IDX_DIM = 64
DSA_TOPK_MAX = 256

B_HEADS = 8
B_KV_GROUPS = 2
NSA_CMP_LEN = 32
NSA_CMP_STRIDE = 16
NSA_SLC_LEN = 64
NSA_SLC_N = 16
NSA_WINDOW = 512

CONV_CH = 1024
CONV_W = 3

SSD_HEAD_DIM = 64
SSD_HEADS = 48
SSD_INNER = SSD_HEADS * SSD_HEAD_DIM
SSD_GROUPS = 8
SSD_STATE = 128
SSD_CONV_W = 4
SSD_CHUNK = 128
SSD_REP = SSD_HEADS // SSD_GROUPS

PEER_HEADS = 8
PEER_NKEYS = 128
PEER_N = PEER_NKEYS * PEER_NKEYS
PEER_DKEY = 256
PEER_TOPK_HALF = 16
PEER_TOPK = 16

DEPTH = 2
ALPHA = (2 * DEPTH) ** 0.25

LANES = 128
SUBLANES = 8
VMEM_LIMIT = 56 * 1024 * 1024

L0_Q, L0_IQ, L0_BQ, L0_KV, L0_CKV, L0_IK, L0_IW, L0_GATE = 0, 1024, 2048, 3072, 4608, 4864, 4992, 5120
NP0 = 5376
L1_Z, L1_XS, L1_BS, L1_CS, L1_DT = 3072, 6144, 9216, 10240, 11264
NP1 = L1_DT + SSD_GROUPS * LANES


def _cp(*sem):
    return pltpu.CompilerParams(dimension_semantics=sem, vmem_limit_bytes=VMEM_LIMIT)


def _dot(a, b):
    return jnp.dot(a, b, preferred_element_type=F32)


def _dot_nt(a, b):
    return lax.dot_general(a, b, (((1,), (1,)), ((), ())), preferred_element_type=F32)


def _dot_tn(a, b):
    return lax.dot_general(a, b, (((0,), (0,)), ((), ())), preferred_element_type=F32)


def _softmax_rows(s):
    m = jnp.max(s, axis=-1, keepdims=True)
    p = jnp.exp(s - m)
    return p / jnp.sum(p, axis=-1, keepdims=True)


def _layer_norm(v, g, b):
    mu = jnp.mean(v, axis=-1, keepdims=True)
    d = v - mu
    var = jnp.mean(d * d, axis=-1, keepdims=True)
    return d * lax.rsqrt(var + EPS) * g + b


def _mm_kernel(x_ref, w_ref, o_ref, xb_ref):
    @pl.when(pl.program_id(1) == 0)
    def _():
        xb_ref[...] = x_ref[...].astype(BF16)

    o_ref[...] = _dot(xb_ref[...], w_ref[...])


def _matmul(x, w, tm=512, tn=768):
    t, k = x.shape
    n = w.shape[1]
    return pl.pallas_call(
        _mm_kernel,
        grid=(t // tm, n // tn),
        in_specs=[pl.BlockSpec((tm, k), lambda i, j: (i, 0)),
                  pl.BlockSpec((k, tn), lambda i, j: (0, j))],
        out_specs=pl.BlockSpec((tm, tn), lambda i, j: (i, j)),
        out_shape=jax.ShapeDtypeStruct((t, n), F32),
        scratch_shapes=[pltpu.VMEM((tm, k), BF16)],
        compiler_params=_cp("arbitrary", "arbitrary"),
        name="in_proj",
    )(x, w)


def _dsa_kernel(q_ref, iq_ref, iw_ref, ckv_ref, ik_ref, gain_ref, wuk_ref, wuv_ref, o_ref,
                ckvn_ref, iklo_ref, ikhi_ref, key_ref, bias_ref, *, seq, topk):
    i = pl.program_id(1)

    @pl.when(i == 0)
    def _():
        c = ckv_ref[...]
        ms = jnp.mean(c * c, axis=-1, keepdims=True)
        ckvn_ref[...] = (c * lax.rsqrt(ms + EPS) * gain_ref[...]).astype(BF16)
        ik = ik_ref[...]
        iklo_ref[...] = ik.astype(BF16)
        ikhi_ref[...] = pltpu.roll(ik, IDX_DIM, axis=1).astype(BF16)

    iw = iw_ref[...] * ((IDX_DIM * IDX_HEADS) ** -0.5)
    score = jnp.zeros((Q_BLOCK, seq), F32)
    for p in range(IDX_HEADS // 2):
        iqp = iq_ref[:, p * LANES:(p + 1) * LANES].astype(BF16)
        d_lo = _dot_nt(iqp, iklo_ref[...])
        d_hi = _dot_nt(iqp, ikhi_ref[...])
        score = score + jnp.maximum(d_lo, 0.0) * iw[:, 2 * p:2 * p + 1]
        score = score + jnp.maximum(d_hi, 0.0) * iw[:, 2 * p + 1:2 * p + 2]
    kpos = lax.broadcasted_iota(jnp.int32, (Q_BLOCK, seq), 1)
    qpos = i * Q_BLOCK + lax.broadcasted_iota(jnp.int32, (Q_BLOCK, seq), 0)
    causal = kpos <= qpos
    score = jnp.where(causal, score, NEG)
    bits = lax.bitcast_convert_type(score, jnp.int32)
    key_ref[...] = jnp.where(bits < 0, bits ^ jnp.int32(0x7FFFFFFF), bits)

    def count_ge(c):
        return jnp.sum(jnp.where(key_ref[...] >= c, 1.0, 0.0), axis=1, keepdims=True)

    kf = float(topk)
    zero = jnp.zeros((Q_BLOCK, 1), jnp.int32)
    prefix = jnp.where(count_ge(zero) >= kf, zero, jnp.int32(-2 ** 31))

    def body(t, prefix):
        cand = prefix + lax.shift_left(jnp.int32(1), 30 - t)
        return jnp.where(count_ge(cand) >= kf, cand, prefix)

    prefix = lax.fori_loop(0, 31, body, prefix)
    bias_ref[...] = jnp.where(causal, jnp.where(key_ref[...] >= prefix, 0.0, NEG), NEG)

    scale = HEAD_DIM ** -0.5
    for h in range(A_HEADS):
        qh = q_ref[:, h * HEAD_DIM:(h + 1) * HEAD_DIM].astype(BF16)
        ql = _dot_nt(qh, wuk_ref[h]) * scale
        s = _dot_nt(ql.astype(BF16), ckvn_ref[...]) + bias_ref[...]
        p = _softmax_rows(s)
        ol = _dot(p.astype(BF16), ckvn_ref[...])
        o_ref[:, h * HEAD_DIM:(h + 1) * HEAD_DIM] = _dot(ol.astype(BF16), wuv_ref[h])


def _dsa(y0, kv_gain, w_uk, w_uv):
    bsz, seq, _ = y0.shape
    topk = min(DSA_TOPK_MAX, seq // 4)
    nb = seq // Q_BLOCK
    kern = functools.partial(_dsa_kernel, seq=seq, topk=topk)
    return pl.pallas_call(
        kern,
        grid=(bsz, nb),
        in_specs=[
            pl.BlockSpec((None, Q_BLOCK, 1024), lambda b, i: (b, i, L0_Q // 1024)),
            pl.BlockSpec((None, Q_BLOCK, 1024), lambda b, i: (b, i, L0_IQ // 1024)),
            pl.BlockSpec((None, Q_BLOCK, LANES), lambda b, i: (b, i, L0_IW // LANES)),
            pl.BlockSpec((None, seq, A_KV_RANK), lambda b, i: (b, 0, L0_CKV // A_KV_RANK)),
            pl.BlockSpec((None, seq, LANES), lambda b, i: (b, 0, L0_IK // LANES)),
            pl.BlockSpec((1, A_KV_RANK), lambda b, i: (0, 0)),
            pl.BlockSpec((A_HEADS, A_KV_RANK, HEAD_DIM), lambda b, i: (0, 0, 0)),
            pl.BlockSpec((A_HEADS, A_KV_RANK, HEAD_DIM), lambda b, i: (0, 0, 0)),
        ],
        out_specs=pl.BlockSpec((None, Q_BLOCK, A_HEADS * HEAD_DIM), lambda b, i: (b, i, 0)),
        out_shape=jax.ShapeDtypeStruct((bsz, seq, A_HEADS * HEAD_DIM), F32),
        scratch_shapes=[pltpu.VMEM((seq, A_KV_RANK), BF16), pltpu.VMEM((seq, LANES), BF16),
                        pltpu.VMEM((seq, LANES), BF16), pltpu.VMEM((Q_BLOCK, seq), jnp.int32),
                        pltpu.VMEM((Q_BLOCK, seq), F32)],
        compiler_params=_cp("arbitrary", "arbitrary"),
        name="dsa",
    )(y0, y0, y0, y0, y0, kv_gain.reshape(1, -1), w_uk.astype(BF16), w_uv.astype(BF16))


def _nsa_cmp_kernel(x_ref, pos_ref, w1_ref, w2_ref, o_ref, *, nblk):
    half = NSA_CMP_LEN // 2
    a = jnp.zeros((nblk, HEAD_DIM), F32)
    b = jnp.zeros((nblk, HEAD_DIM), F32)
    for l in range(half):
        rows = x_ref[pl.ds(l, nblk, stride=NSA_CMP_STRIDE), :]
        a = a + _dot((rows + pos_ref[l:l + 1, :]).astype(BF16), w1_ref[l].astype(BF16))
        b = b + _dot((rows + pos_ref[l + half:l + half + 1, :]).astype(BF16),
                     w1_ref[l + half].astype(BF16))
    h = jax.nn.gelu(a + pltpu.roll(b, nblk - 1, axis=0))
    o_ref[...] = _dot(h.astype(BF16), w2_ref[...].astype(BF16))


def _nsa_compress(y0, pos, w1, w2):
    bsz, seq, _ = y0.shape
    nblk = seq // NSA_CMP_STRIDE
    kern = functools.partial(_nsa_cmp_kernel, nblk=nblk)
    return pl.pallas_call(
        kern,
        grid=(bsz, 2, B_KV_GROUPS),
        in_specs=[
            pl.BlockSpec((None, seq, HEAD_DIM), lambda b, t, g: (b, 0, L0_KV // HEAD_DIM + 2 * t + g)),
            pl.BlockSpec((None, NSA_CMP_LEN, HEAD_DIM), lambda b, t, g: (t, 0, 0)),
            pl.BlockSpec((None, NSA_CMP_LEN, HEAD_DIM, HEAD_DIM), lambda b, t, g: (t, 0, 0, 0)),
            pl.BlockSpec((None, HEAD_DIM, HEAD_DIM), lambda b, t, g: (t, 0, 0)),
        ],
        out_specs=pl.BlockSpec((None, None, None, nblk, HEAD_DIM), lambda b, t, g: (b, t, g, 0, 0)),
        out_shape=jax.ShapeDtypeStruct((bsz, 2, B_KV_GROUPS, nblk, HEAD_DIM), F32),
        compiler_params=_cp("arbitrary", "arbitrary", "arbitrary"),
        name="nsa_compress",
    )(y0, pos, w1, w2)


def _nsa_kernel(q_ref, gate_ref, kc_ref, vc_ref, ks_ref, vs_ref, kw_ref, vw_ref, ovl_ref, exp_ref,
                o_ref, ksb_ref, vsb_ref, kwb_ref, vwb_ref, bias_ref, *, seq, nsel):
    i = pl.program_id(2)
    hpg = B_HEADS // B_KV_GROUPS
    rows = hpg * Q_BLOCK
    ncmp = kc_ref.shape[0]
    wlen = NSA_WINDOW + Q_BLOCK

    @pl.when(i == 0)
    def _():
        ksb_ref[...] = ks_ref[...].astype(BF16)
        vsb_ref[...] = vs_ref[...].astype(BF16)
        kwb_ref[...] = kw_ref[...].astype(BF16)
        vwb_ref[...] = vw_ref[...].astype(BF16)

    start = i * Q_BLOCK
    scale = HEAD_DIM ** -0.5
    q4 = jnp.concatenate([q_ref[:, h * HEAD_DIM:(h + 1) * HEAD_DIM] for h in range(hpg)], axis=0)
    q4 = (q4 * scale).astype(BF16)
    qrow = lax.broadcasted_iota(jnp.int32, (Q_BLOCK, 1), 0) + start
    qrow4 = jnp.concatenate([qrow] * hpg, axis=0)

    s_c = _dot_nt(q4, kc_ref[...].astype(BF16))
    cend = lax.broadcasted_iota(jnp.int32, (1, ncmp), 1) * NSA_CMP_STRIDE + (NSA_CMP_LEN - 1)
    vis_c = cend <= qrow4
    p_c = _softmax_rows(jnp.where(vis_c, s_c, NEG))
    p_c = jnp.where(qrow4 >= NSA_CMP_LEN - 1, p_c, 0.0)
    o_c = _dot(p_c.astype(BF16), vc_ref[...].astype(BF16))

    psum = p_c[0:Q_BLOCK]
    for h in range(1, hpg):
        psum = psum + p_c[h * Q_BLOCK:(h + 1) * Q_BLOCK]
    imp = jnp.dot(psum, ovl_ref[...], preferred_element_type=F32, precision=lax.Precision.HIGHEST)
    nslc = seq // NSA_SLC_LEN
    jj = lax.broadcasted_iota(jnp.int32, (1, LANES), 1)
    cur = qrow // NSA_SLC_LEN
    forced = (jj == 0) | (jj == cur) | (jj == cur - 1)
    imp = jnp.where(forced, FORCE, imp)
    imp = jnp.where(jj <= cur, imp, NEG)
    imp = jnp.where(jj < nslc, imp, -3e38)
    rank = jnp.zeros((Q_BLOCK, LANES), F32)
    for t in range(nslc):
        col = imp[:, t:t + 1]
        later = jnp.where(jj > t, 1.0, 0.0)
        rank = rank + jnp.where(col > imp, 1.0, jnp.where(col == imp, later, 0.0))
    selb = jnp.where(rank < float(nsel), 1.0, 0.0).astype(BF16)
    keysel = _dot(selb, exp_ref[...])
    kpos = lax.broadcasted_iota(jnp.int32, (1, seq), 1)
    bias_ref[...] = jnp.where((keysel > 0.5) & (kpos <= qrow), 0.0, NEG)

    ws = pl.multiple_of(jnp.maximum(start - NSA_WINDOW, 0), Q_BLOCK)
    kposw = ws + lax.broadcasted_iota(jnp.int32, (1, wlen), 1)
    vis_w = (kposw <= qrow4) & (kposw > qrow4 - NSA_WINDOW)
    s_w = _dot_nt(q4, kwb_ref[pl.ds(ws, wlen), :])
    p_w = _softmax_rows(jnp.where(vis_w, s_w, NEG))
    o_w = _dot(p_w.astype(BF16), vwb_ref[pl.ds(ws, wlen), :])

    gate = jax.nn.sigmoid(gate_ref[...])
    for h in range(hpg):
        r0, r1 = h * Q_BLOCK, (h + 1) * Q_BLOCK
        s_s = _dot_nt(q4[r0:r1], ksb_ref[...]) + bias_ref[...]
        p_s = _softmax_rows(s_s)
        o_s = _dot(p_s.astype(BF16), vsb_ref[...])
        o = (gate[:, 3 * h:3 * h + 1] * o_c[r0:r1] + gate[:, 3 * h + 1:3 * h + 2] * o_s
             + gate[:, 3 * h + 2:3 * h + 3] * o_w[r0:r1])
        o_ref[:, h * HEAD_DIM:(h + 1) * HEAD_DIM] = o


def _nsa(y0, kvc):
    bsz, seq, _ = y0.shape
    nb = seq // Q_BLOCK
    nslc = seq // NSA_SLC_LEN
    nsel = min(NSA_SLC_N, nslc)
    ncmp = seq // NSA_CMP_STRIDE
    hpg = B_HEADS // B_KV_GROUPS
    cs = np.arange(ncmp)[:, None] * NSA_CMP_STRIDE
    ss = np.arange(LANES)[None, :] * NSA_SLC_LEN
    ovl = ((cs < ss + NSA_SLC_LEN) & (cs + NSA_CMP_LEN > ss) & (np.arange(ncmp)[:, None] < ncmp - 1)
           & (np.arange(LANES)[None, :] < nslc)).astype(np.float32)
    expand = (np.arange(seq)[None, :] // NSA_SLC_LEN == np.arange(LANES)[:, None]).astype(np.float32)
    kvb = L0_KV // HEAD_DIM
    full = lambda off: pl.BlockSpec((None, seq, HEAD_DIM), lambda b, g, i, off=off: (b, 0, kvb + off + g))
    kern = functools.partial(_nsa_kernel, seq=seq, nsel=nsel)
    return pl.pallas_call(
        kern,
        grid=(bsz, B_KV_GROUPS, nb),
        in_specs=[
            pl.BlockSpec((None, Q_BLOCK, hpg * HEAD_DIM), lambda b, g, i: (b, i, L0_BQ // (hpg * HEAD_DIM) + g)),
            pl.BlockSpec((None, Q_BLOCK, LANES), lambda b, g, i: (b, i, L0_GATE // LANES + g)),
            pl.BlockSpec((None, None, None, ncmp, HEAD_DIM), lambda b, g, i: (b, 0, g, 0, 0)),
            pl.BlockSpec((None, None, None, ncmp, HEAD_DIM), lambda b, g, i: (b, 1, g, 0, 0)),
            full(4), full(6), full(8), full(10),
            pl.BlockSpec((ncmp, LANES), lambda b, g, i: (0, 0)),
            pl.BlockSpec((LANES, seq), lambda b, g, i: (0, 0)),
        ],
        out_specs=pl.BlockSpec((None, Q_BLOCK, hpg * HEAD_DIM), lambda b, g, i: (b, i, g)),
        out_shape=jax.ShapeDtypeStruct((bsz, seq, B_HEADS * HEAD_DIM), F32),
        scratch_shapes=[pltpu.VMEM((seq, HEAD_DIM), BF16)] * 4 + [pltpu.VMEM((Q_BLOCK, seq), F32)],
        compiler_params=_cp("arbitrary", "arbitrary", "arbitrary"),
        name="nsa",
    )(y0, y0, kvc, kvc, y0, y0, y0, y0, jnp.asarray(ovl), jnp.asarray(expand, dtype=BF16))


def _out_ln_kernel(a_ref, b_ref, wa_ref, wb_ref, x_ref, g_ref, beta_ref, o_ref):
    acc = _dot(a_ref[...].astype(BF16), wa_ref[...]) + _dot(b_ref[...].astype(BF16), wb_ref[...])
    o_ref[...] = _layer_norm(ALPHA * x_ref[...] + acc, g_ref[...], beta_ref[...])


def _out_ln(a, b, w_out, x, g, beta, tm=256):
    t, ka = a.shape
    kb = b.shape[1]
    d = x.shape[1]
    wa = w_out[:ka].astype(BF16)
    wb = w_out[ka:].astype(BF16)
    return pl.pallas_call(
        _out_ln_kernel,
        grid=(t // tm,),
        in_specs=[pl.BlockSpec((tm, ka), lambda i: (i, 0)), pl.BlockSpec((tm, kb), lambda i: (i, 0)),
                  pl.BlockSpec((ka, d), lambda i: (0, 0)), pl.BlockSpec((kb, d), lambda i: (0, 0)),
                  pl.BlockSpec((tm, d), lambda i: (i, 0)),
                  pl.BlockSpec((1, d), lambda i: (0, 0)), pl.BlockSpec((1, d), lambda i: (0, 0))],
        out_specs=pl.BlockSpec((tm, d), lambda i: (i, 0)),
        out_shape=jax.ShapeDtypeStruct((t, d), F32),
        compiler_params=_cp("arbitrary"),
        name="out_proj_ln",
    )(a, b, wa, wb, x, g.reshape(1, -1), beta.reshape(1, -1))


BIG = 1e30


def _top_values(s, n):
    dist, cum = [], []
    cur = s
    total = jnp.zeros((1, s.shape[1]), F32)
    for _ in range(n):
        m = jnp.max(cur, axis=0, keepdims=True)
        eq = cur == m
        total = total + jnp.sum(jnp.where(eq, 1.0, 0.0), axis=0, keepdims=True)
        dist.append(m)
        cum.append(total)
        cur = jnp.where(eq, -BIG, cur)
    vals = []
    for k in range(1, n + 1):
        v = dist[n - 1]
        for m in range(n - 2, -1, -1):
            v = jnp.where(cum[m] >= float(k), dist[m], v)
        vals.append(v)
    return vals


def _peer_stats_kernel(x_ref, wqt_ref, keys_ref, b1_ref, c1_ref, a2_ref, e2_ref, qt_ref):
    qt_ref[...] = _dot_nt(wqt_ref[...], x_ref[...].astype(BF16))
    half = PEER_DKEY // 2
    for h in range(PEER_HEADS):
        r0 = h * PEER_DKEY
        s1 = _dot(keys_ref[0], qt_ref[r0:r0 + half, :].astype(BF16))
        s2 = _dot(keys_ref[1], qt_ref[r0 + half:r0 + PEER_DKEY, :].astype(BF16))
        v1 = _top_values(s1, PEER_TOPK_HALF)
        v2 = _top_values(s2, PEER_TOPK_HALF)
        sv2 = jnp.concatenate(v2, axis=0)
        cand = jnp.concatenate([a + sv2 for a in v1], axis=0)
        cv = _top_values(cand, PEER_TOPK + 1)
        top, kth, nxt = cv[0], cv[PEER_TOPK - 1], cv[PEER_TOPK]
        z = jnp.sum(jnp.where(cand >= kth, jnp.exp(cand - top), 0.0), axis=0, keepdims=True)
        thr = 0.5 * (kth + nxt)
        act1 = s1 >= v1[PEER_TOPK_HALF - 1]
        act2 = s2 >= v2[PEER_TOPK_HALF - 1]
        b1_ref[h] = jnp.where(act1, thr - s1, BIG)
        c1_ref[h] = jnp.where(act1, jnp.exp(s1 - v1[0]) / z, 0.0)
        a2_ref[h] = jnp.where(act2, s2, -BIG)
        e2_ref[h] = jnp.where(act2, jnp.exp(s2 - v2[0]), 0.0)


def _peer_stats(x, w_q, sub_keys, tn=512):
    t, d = x.shape
    wqt = w_q.T.astype(BF16)
    shp = jax.ShapeDtypeStruct((PEER_HEADS, PEER_NKEYS, t), F32)
    spec = pl.BlockSpec((PEER_HEADS, PEER_NKEYS, tn), lambda i: (0, 0, i))
    return pl.pallas_call(
        _peer_stats_kernel,
        grid=(t // tn,),
        in_specs=[pl.BlockSpec((tn, d), lambda i: (i, 0)),
                  pl.BlockSpec((PEER_HEADS * PEER_DKEY, d), lambda i: (0, 0)),
                  pl.BlockSpec((2, PEER_NKEYS, PEER_DKEY // 2), lambda i: (0, 0, 0))],
        out_specs=[spec] * 4,
        out_shape=[shp] * 4,
        scratch_shapes=[pltpu.VMEM((PEER_HEADS * PEER_DKEY, tn), F32)],
        compiler_params=_cp("arbitrary"),
        name="peer_stats",
    )(x, wqt, sub_keys.astype(BF16))


def _peer_kernel(x_ref, u_ref, vt_ref, b1_ref, c1_ref, a2_ref, e2_ref, g_ref, beta_ref, o_ref,
                 xb_ref, acc_ref, w_ref, *, te):
    j = pl.program_id(1)

    @pl.when(j == 0)
    def _():
        xb_ref[...] = x_ref[...].astype(BF16)
        acc_ref[...] = jnp.zeros_like(acc_ref)

    act = jax.nn.gelu(_dot_nt(u_ref[...], xb_ref[...]))
    nsub = te // PEER_NKEYS
    for ii in range(nsub):
        idx = j * nsub + ii
        gate = jnp.zeros((PEER_NKEYS, act.shape[1]), F32)
        for h in range(PEER_HEADS):
            b1 = b1_ref[h, pl.ds(idx, 1), :]
            c1 = c1_ref[h, pl.ds(idx, 1), :]
            gate = gate + jnp.where(a2_ref[h] >= b1, e2_ref[h], 0.0) * c1
        w_ref[ii * PEER_NKEYS:(ii + 1) * PEER_NKEYS, :] = (
            act[ii * PEER_NKEYS:(ii + 1) * PEER_NKEYS, :] * gate).astype(BF16)
    acc_ref[...] += _dot(vt_ref[...], w_ref[...])

    @pl.when(j == pl.num_programs(1) - 1)
    def _():
        y = ALPHA * x_ref[...] + acc_ref[...].T
        o_ref[...] = _layer_norm(y, g_ref[...], beta_ref[...])


def _peer_ln(x, stats, u_tab, v_tab, g, beta, tn=512, te=256):
    t, d = x.shape
    ne = u_tab.shape[0]
    ub = u_tab.astype(BF16)
    vtb = v_tab.T.astype(BF16)
    sspec = pl.BlockSpec((PEER_HEADS, PEER_NKEYS, tn), lambda i, j: (0, 0, i))
    kern = functools.partial(_peer_kernel, te=te)
    return pl.pallas_call(
        kern,
        grid=(t // tn, ne // te),
        in_specs=[pl.BlockSpec((tn, d), lambda i, j: (i, 0)),
                  pl.BlockSpec((te, d), lambda i, j: (j, 0)),
                  pl.BlockSpec((d, te), lambda i, j: (0, j)),
                  sspec, sspec, sspec, sspec,
                  pl.BlockSpec((1, d), lambda i, j: (0, 0)), pl.BlockSpec((1, d), lambda i, j: (0, 0))],
        out_specs=pl.BlockSpec((tn, d), lambda i, j: (i, 0)),
        out_shape=jax.ShapeDtypeStruct((t, d), F32),
        scratch_shapes=[pltpu.VMEM((tn, d), BF16), pltpu.VMEM((d, tn), F32), pltpu.VMEM((te, tn), BF16)],
        compiler_params=_cp("arbitrary", "arbitrary"),
        name="peer",
    )(x, ub, vtb, *stats, g.reshape(1, -1), beta.reshape(1, -1))


def _shortconv_kernel(cb_ref, cc_ref, ch_ref, ccp_ref, chp_ref, w_ref, o_ref, ext_ref, *, ts):
    i = pl.program_id(1)
    prev = ccp_ref[...] * chp_ref[...]
    ext_ref[0:SUBLANES, :] = jnp.where(i > 0, prev, 0.0)
    ext_ref[SUBLANES:, :] = cc_ref[...] * ch_ref[...]
    acc = jnp.zeros(o_ref.shape, F32)
    for k in range(CONV_W):
        off = SUBLANES - (CONV_W - 1) + k
        acc = acc + w_ref[k:k + 1, :] * ext_ref[off:off + ts, :]
    o_ref[...] = cb_ref[...] * acc


def _shortconv(y1, conv_w, ts=512):
    bsz, seq, _ = y1.shape
    blk = lambda c: pl.BlockSpec((None, ts, CONV_CH), lambda b, i, c=c: (b, i, c))
    halo = lambda c: pl.BlockSpec((None, SUBLANES, CONV_CH),
                                  lambda b, i, c=c: (b, jnp.maximum(i * (ts // SUBLANES) - 1, 0), c))
    kern = functools.partial(_shortconv_kernel, ts=ts)
    return pl.pallas_call(
        kern,
        grid=(bsz, seq // ts),
        in_specs=[blk(0), blk(1), blk(2), halo(1), halo(2),
                  pl.BlockSpec((CONV_W, CONV_CH), lambda b, i: (0, 0))],
        out_specs=pl.BlockSpec((None, ts, CONV_CH), lambda b, i: (b, i, 0)),
        out_shape=jax.ShapeDtypeStruct((bsz, seq, CONV_CH), F32),
        scratch_shapes=[pltpu.VMEM((ts + SUBLANES, CONV_CH), F32)],
        compiler_params=_cp("arbitrary", "arbitrary"),
        name="short_conv",
    )(y1, y1, y1, y1, y1, conv_w)


def _ssdconv_kernel(u_ref, up_ref, w_ref, b_ref, o_ref, ext_ref, *, ts):
    i = pl.program_id(1)
    ext_ref[0:SUBLANES, :] = jnp.where(i > 0, up_ref[...], 0.0)
    ext_ref[SUBLANES:, :] = u_ref[...]
    acc = jnp.zeros(o_ref.shape, F32) + b_ref[...]
    for k in range(SSD_CONV_W):
        off = SUBLANES - (SSD_CONV_W - 1) + k
        acc = acc + w_ref[k:k + 1, :] * ext_ref[off:off + ts, :]
    o_ref[...] = acc * jax.nn.sigmoid(acc)


def _ssdconv(y1, w, bias, ts=512, tc=1024):
    bsz, seq, _ = y1.shape
    nch = w.shape[1]
    c0 = L1_XS // tc
    kern = functools.partial(_ssdconv_kernel, ts=ts)
    return pl.pallas_call(
        kern,
        grid=(bsz, seq // ts, nch // tc),
        in_specs=[pl.BlockSpec((None, ts, tc), lambda b, i, c: (b, i, c0 + c)),
                  pl.BlockSpec((None, SUBLANES, tc),
                               lambda b, i, c: (b, jnp.maximum(i * (ts // SUBLANES) - 1, 0), c0 + c)),
                  pl.BlockSpec((SSD_CONV_W, tc), lambda b, i, c: (0, c)),
                  pl.BlockSpec((1, tc), lambda b, i, c: (0, c))],
        out_specs=pl.BlockSpec((None, ts, tc), lambda b, i, c: (b, i, c)),
        out_shape=jax.ShapeDtypeStruct((bsz, seq, nch), F32),
        scratch_shapes=[pltpu.VMEM((ts + SUBLANES, tc), F32)],
        compiler_params=_cp("arbitrary", "arbitrary", "arbitrary"),
        name="ssd_conv",
    )(y1, y1, w, bias.reshape(1, -1))


def _ssd_kernel(xs_ref, bs_ref, cs_ref, z_ref, dt_ref, dtb_ref, aneg_ref, dskip_ref, ng_ref, o_ref,
                state_ref):
    c = pl.program_id(2)
    lc = SSD_CHUNK
    npair = SSD_REP // 2

    @pl.when(c == 0)
    def _():
        state_ref[...] = jnp.zeros_like(state_ref)

    dt = jax.nn.softplus(dt_ref[...] + dtb_ref[...])
    da = dt * aneg_ref[...]
    ri = lax.broadcasted_iota(jnp.int32, (lc, lc), 0)
    ci = lax.broadcasted_iota(jnp.int32, (lc, lc), 1)
    causal = ci <= ri
    tril = jnp.where(causal, 1.0, 0.0)
    a_cs = jnp.dot(tril, da, preferred_element_type=F32, precision=lax.Precision.HIGHEST)
    a_cs_t = a_cs.T
    a_last = a_cs[lc - 1:lc, :]
    decay_s = jnp.exp(a_last - a_cs)
    decay_o = jnp.exp(a_cs)
    chunk_decay = jnp.exp(a_last)

    bc = bs_ref[...].astype(BF16)
    cc = cs_ref[...].astype(BF16)
    cb = _dot_nt(cc, bc)
    lane = lax.broadcasted_iota(jnp.int32, (lc, LANES), 1)
    lo_half = lane < SSD_HEAD_DIM
    row_lo = lax.broadcasted_iota(jnp.int32, (LANES, 1), 0) < SSD_HEAD_DIM

    hz = []
    ssq = jnp.zeros((lc, 1), F32)
    for pr in range(npair):
        r0, r1 = 2 * pr, 2 * pr + 1
        sl = slice(pr * LANES, (pr + 1) * LANES)
        x = xs_ref[:, sl]
        pick = lambda m: jnp.where(lo_half, m[:, r0:r0 + 1], m[:, r1:r1 + 1])
        xdt = x * pick(dt)
        y = jnp.zeros((lc, LANES), F32)
        for r, keep in ((r0, lo_half), (r1, ~lo_half)):
            seg = a_cs[:, r:r + 1] - a_cs_t[r:r + 1, :]
            m = cb * jnp.exp(jnp.where(causal, seg, NEG))
            y = y + _dot(m.astype(BF16), jnp.where(keep, xdt, 0.0).astype(BF16))
        prev = state_ref[sl, :]
        y = y + _dot_nt(cc, prev.astype(BF16)) * pick(decay_o)
        st = _dot_tn((xdt * pick(decay_s)).astype(BF16), bc)
        cd = jnp.where(row_lo, chunk_decay[:, r0:r0 + 1], chunk_decay[:, r1:r1 + 1])
        state_ref[sl, :] = prev * cd + st
        y = y + jnp.where(lo_half, dskip_ref[:, r0:r0 + 1], dskip_ref[:, r1:r1 + 1]) * x
        z = z_ref[:, sl]
        h = y * (z * jax.nn.sigmoid(z))
        hz.append(h)
        ssq = ssq + jnp.sum(h * h, axis=-1, keepdims=True)
    inv = lax.rsqrt(ssq / (SSD_REP * SSD_HEAD_DIM) + EPS)
    for pr in range(npair):
        sl = slice(pr * LANES, (pr + 1) * LANES)
        o_ref[:, sl] = hz[pr] * inv * ng_ref[:, sl]


def _ssd(y1, xbc, dt_bias_g, a_neg_g, d_skip_g, norm_g):
    bsz, seq, _ = y1.shape
    gw = SSD_REP * SSD_HEAD_DIM
    nchunk = seq // SSD_CHUNK
    vec = pl.BlockSpec((None, 1, LANES), lambda b, g, c: (g, 0, 0))
    return pl.pallas_call(
        _ssd_kernel,
        grid=(bsz, SSD_GROUPS, nchunk),
        in_specs=[
            pl.BlockSpec((None, SSD_CHUNK, gw), lambda b, g, c: (b, c, g)),
            pl.BlockSpec((None, SSD_CHUNK, SSD_STATE), lambda b, g, c: (b, c, SSD_INNER // SSD_STATE + g)),
            pl.BlockSpec((None, SSD_CHUNK, SSD_STATE),
                         lambda b, g, c: (b, c, SSD_INNER // SSD_STATE + SSD_GROUPS + g)),
            pl.BlockSpec((None, SSD_CHUNK, gw), lambda b, g, c: (b, c, L1_Z // gw + g)),
            pl.BlockSpec((None, SSD_CHUNK, LANES), lambda b, g, c: (b, c, L1_DT // LANES + g)),
            vec, vec, vec,
            pl.BlockSpec((1, gw), lambda b, g, c: (0, g)),
        ],
        out_specs=pl.BlockSpec((None, SSD_CHUNK, gw), lambda b, g, c: (b, c, g)),
        out_shape=jax.ShapeDtypeStruct((bsz, seq, SSD_INNER), F32),
        scratch_shapes=[pltpu.VMEM((gw, SSD_STATE), F32)],
        compiler_params=_cp("arbitrary", "arbitrary", "arbitrary"),
        name="ssd",
    )(xbc, xbc, xbc, y1, y1, dt_bias_g, a_neg_g, d_skip_g, norm_g.reshape(1, -1))


def _pad_cols(w, n):
    return jnp.pad(w, ((0, 0), (0, n - w.shape[1])))


def _pack_w_in0(w):
    o = np.cumsum((0, A_HEADS * HEAD_DIM, A_KV_RANK, IDX_HEADS * IDX_DIM, IDX_DIM, IDX_HEADS,
                   B_HEADS * HEAD_DIM) + (B_KV_GROUPS * HEAD_DIM,) * 6 + (B_HEADS * 3,))
    seg = lambda k: w[:, int(o[k]):int(o[k + 1])]
    gates = seg(12)
    ng = gates.shape[1] // B_KV_GROUPS
    parts = [seg(0), seg(2), seg(5)] + [seg(k) for k in range(6, 12)] + [
        seg(1), _pad_cols(seg(3), LANES), _pad_cols(seg(4), LANES)] + [
        _pad_cols(gates[:, g * ng:(g + 1) * ng], LANES) for g in range(B_KV_GROUPS)]
    return jnp.concatenate(parts, axis=1).astype(BF16)


def _pack_w_in1(w):
    dt = w[:, L1_DT:]
    parts = [w[:, :L1_DT]] + [_pad_cols(dt[:, g * SSD_REP:(g + 1) * SSD_REP], LANES) for g in range(SSD_GROUPS)]
    return jnp.concatenate(parts, axis=1).astype(BF16)


def _per_group(v):
    return jnp.pad(v.reshape(SSD_GROUPS, 1, SSD_REP), ((0, 0), (0, 0), (0, LANES - SSD_REP)))


def _peer_block(x, wq, keys, u, v, g, b):
    stats = _peer_stats(x, wq, keys)
    return _peer_ln(x, stats, u, v, g, b)


def _even_layer(x, bsz, seq, w_in, kv_gain, w_uk, w_uv, cmpk_pos, cmpk_w1, cmpk_w2, cmpv_pos, cmpv_w1, cmpv_w2,
                w_out, ln1_g, ln1_b, peer_wq, peer_keys, peer_u, peer_v, ln2_g, ln2_b):
    t = bsz * seq
    y0 = _matmul(x, _pack_w_in0(w_in)).reshape(bsz, seq, NP0)
    o_a = _dsa(y0, kv_gain, w_uk, w_uv)
    kvc = _nsa_compress(y0, jnp.stack([cmpk_pos, cmpv_pos]), jnp.stack([cmpk_w1, cmpv_w1]),
                        jnp.stack([cmpk_w2, cmpv_w2]))
    o_b = _nsa(y0, kvc)
    x = _out_ln(o_a.reshape(t, -1), o_b.reshape(t, -1), w_out, x, ln1_g, ln1_b)
    return _peer_block(x, peer_wq, peer_keys, peer_u, peer_v, ln2_g, ln2_b)


def _odd_layer(x, bsz, seq, w_in, conv_w, ssd_conv_w, ssd_conv_b, dt_bias, a_log, d_skip, norm_g,
               w_out, ln1_g, ln1_b, peer_wq, peer_keys, peer_u, peer_v, ln2_g, ln2_b):
    t = bsz * seq
    y1 = _matmul(x, _pack_w_in1(w_in)).reshape(bsz, seq, NP1)
    y_c = _shortconv(y1, conv_w)
    xbc = _ssdconv(y1, ssd_conv_w, ssd_conv_b)
    y_d = _ssd(y1, xbc, _per_group(dt_bias), _per_group(-jnp.exp(a_log)), _per_group(d_skip), norm_g)
    x = _out_ln(y_c.reshape(t, -1), y_d.reshape(t, -1), w_out, x, ln1_g, ln1_b)
    return _peer_block(x, peer_wq, peer_keys, peer_u, peer_v, ln2_g, ln2_b)


def kernel(x, l0_w_in, l0_kv_gain, l0_w_uk, l0_w_uv, l0_cmpk_pos, l0_cmpk_w1, l0_cmpk_w2, l0_cmpv_pos, l0_cmpv_w1, l0_cmpv_w2, l0_w_out, l0_ln1_g, l0_ln1_b, l0_peer_wq, l0_peer_keys, l0_peer_u, l0_peer_v, l0_ln2_g, l0_ln2_b, l1_w_in, l1_conv_w, l1_ssd_conv_w, l1_ssd_conv_b, l1_dt_bias, l1_a_log, l1_d_skip, l1_norm_g, l1_w_out, l1_ln1_g, l1_ln1_b, l1_peer_wq, l1_peer_keys, l1_peer_u, l1_peer_v, l1_ln2_g, l1_ln2_b):
    bsz, seq, d = x.shape
    h = x.reshape(bsz * seq, d)
    h = _even_layer(h, bsz, seq, l0_w_in, l0_kv_gain, l0_w_uk, l0_w_uv, l0_cmpk_pos, l0_cmpk_w1, l0_cmpk_w2,
                    l0_cmpv_pos, l0_cmpv_w1, l0_cmpv_w2, l0_w_out, l0_ln1_g, l0_ln1_b,
                    l0_peer_wq, l0_peer_keys, l0_peer_u, l0_peer_v, l0_ln2_g, l0_ln2_b)
    h = _odd_layer(h, bsz, seq, l1_w_in, l1_conv_w, l1_ssd_conv_w, l1_ssd_conv_b, l1_dt_bias, l1_a_log,
                   l1_d_skip, l1_norm_g, l1_w_out, l1_ln1_g, l1_ln1_b,
                   l1_peer_wq, l1_peer_keys, l1_peer_u, l1_peer_v, l1_ln2_g, l1_ln2_b)
    return h.reshape(bsz, seq, d)
```

```python
import functools

import numpy as np
import jax
import jax.numpy as jnp
from jax import lax
from jax.experimental import pallas as pl
from jax.experimental.pallas import tpu as pltpu

F32 = jnp.float32
BF16 = jnp.bfloat16

D_MODEL = 2048
HEAD_DIM = 128
Q_BLOCK = 128
NEG = -1e30
FORCE = 1e30
EPS = 1e-5

A_HEADS = 8
A_KV_RANK = 256
IDX_HEADS = 16
IDX_DIM = 64
DSA_TOPK_MAX = 256
DSA_KEY_CHUNK = 512

B_HEADS = 8
B_KV_GROUPS = 2
NSA_CMP_LEN = 32
NSA_CMP_STRIDE = 16
NSA_SLC_LEN = 64
NSA_SLC_N = 16
NSA_WINDOW = 512

CONV_CH = 1024
CONV_W = 3

SSD_HEAD_DIM = 64
SSD_HEADS = 48
SSD_INNER = SSD_HEADS * SSD_HEAD_DIM
SSD_GROUPS = 8
SSD_STATE = 128
SSD_CONV_W = 4
SSD_CHUNK = 128
SSD_REP = SSD_HEADS // SSD_GROUPS

PEER_HEADS = 8
PEER_NKEYS = 128
PEER_N = PEER_NKEYS * PEER_NKEYS
PEER_DKEY = 256
PEER_TOPK_HALF = 16
PEER_TOPK = 16

DEPTH = 2
ALPHA = (2 * DEPTH) ** 0.25

LANES = 128
SUBLANES = 8
VMEM_LIMIT = 56 * 1024 * 1024

L0_Q, L0_IQ, L0_BQ, L0_KV, L0_CKV, L0_IK, L0_IW, L0_GATE = 0, 1024, 2048, 3072, 4608, 4864, 4992, 5120
NP0 = 5376
L1_Z, L1_XS, L1_BS, L1_CS, L1_DT = 3072, 6144, 9216, 10240, 11264
NP1 = L1_DT + SSD_GROUPS * LANES


def _cp(*sem):
    return pltpu.CompilerParams(dimension_semantics=sem, vmem_limit_bytes=VMEM_LIMIT)


def _dot(a, b):
    return jnp.dot(a, b, preferred_element_type=F32)


def _dot_nt(a, b):
    return lax.dot_general(a, b, (((1,), (1,)), ((), ())), preferred_element_type=F32)


def _dot_tn(a, b):
    return lax.dot_general(a, b, (((0,), (0,)), ((), ())), preferred_element_type=F32)


def _softmax_rows(s):
    m = jnp.max(s, axis=-1, keepdims=True)
    p = jnp.exp(s - m)
    return p / jnp.sum(p, axis=-1, keepdims=True)


def _layer_norm(v, g, b):
    mu = jnp.mean(v, axis=-1, keepdims=True)
    d = v - mu
    var = jnp.mean(d * d, axis=-1, keepdims=True)
    return d * lax.rsqrt(var + EPS) * g + b


def _mm_kernel(x_ref, w_ref, o_ref, xb_ref):
    @pl.when(pl.program_id(1) == 0)
    def _():
        xb_ref[...] = x_ref[...].astype(BF16)

    o_ref[...] = _dot(xb_ref[...], w_ref[...])


def _matmul(x, w, tm=512, tn=768):
    t, k = x.shape
    n = w.shape[1]
    return pl.pallas_call(
        _mm_kernel,
        grid=(t // tm, n // tn),
        in_specs=[pl.BlockSpec((tm, k), lambda i, j: (i, 0)),
                  pl.BlockSpec((k, tn), lambda i, j: (0, j))],
        out_specs=pl.BlockSpec((tm, tn), lambda i, j: (i, j)),
        out_shape=jax.ShapeDtypeStruct((t, n), F32),
        scratch_shapes=[pltpu.VMEM((tm, k), BF16)],
        compiler_params=_cp("arbitrary", "arbitrary"),
        name="in_proj",
    )(x, w)


def _dsa_kernel(q_ref, iq_ref, iw_ref, ckv_ref, ik_ref, gain_ref, wuk_ref, wuv_ref, o_ref,
                ckvn_ref, iklo_ref, ikhi_ref, key_ref, bias_ref, ql_ref, m_ref, l_ref, acc_ref, *, seq, topk):
    i = pl.program_id(1)

    @pl.when(i == 0)
    def _():
        c = ckv_ref[...]
        ms = jnp.mean(c * c, axis=-1, keepdims=True)
        ckvn_ref[...] = (c * lax.rsqrt(ms + EPS) * gain_ref[...]).astype(BF16)
        ik = ik_ref[...]
        iklo_ref[...] = ik.astype(BF16)
        ikhi_ref[...] = pltpu.roll(ik, IDX_DIM, axis=1).astype(BF16)

    kc = DSA_KEY_CHUNK
    nch = (i * Q_BLOCK + Q_BLOCK + kc - 1) // kc
    qpos = i * Q_BLOCK + lax.broadcasted_iota(jnp.int32, (Q_BLOCK, kc), 0)
    kiota = lax.broadcasted_iota(jnp.int32, (Q_BLOCK, kc), 1)

    def chunk(c):
        return pl.ds(pl.multiple_of(c * kc, kc), kc)

    iw = iw_ref[...] * ((IDX_DIM * IDX_HEADS) ** -0.5)

    def index_chunk(c, carry):
        lo = iklo_ref[chunk(c), :]
        hi = ikhi_ref[chunk(c), :]
        score = jnp.zeros((Q_BLOCK, kc), F32)
        for p in range(IDX_HEADS // 2):
            iqp = iq_ref[:, p * LANES:(p + 1) * LANES].astype(BF16)
            score = score + jnp.maximum(_dot_nt(iqp, lo), 0.0) * iw[:, 2 * p:2 * p + 1]
            score = score + jnp.maximum(_dot_nt(iqp, hi), 0.0) * iw[:, 2 * p + 1:2 * p + 2]
        score = jnp.where(c * kc + kiota <= qpos, score, NEG)
        bits = lax.bitcast_convert_type(score, jnp.int32)
        key_ref[:, chunk(c)] = jnp.where(bits < 0, bits ^ jnp.int32(0x7FFFFFFF), bits)
        return carry

    lax.fori_loop(0, nch, index_chunk, 0)

    def count_ge(cand):
        def body(c, acc):
            ge = jnp.where(key_ref[:, chunk(c)] >= cand, 1.0, 0.0)
            for s in range(kc // LANES):
                acc = acc + ge[:, s * LANES:(s + 1) * LANES]
            return acc
        acc = lax.fori_loop(0, nch, body, jnp.zeros((Q_BLOCK, LANES), F32))
        return jnp.sum(acc, axis=1, keepdims=True)

    kf = float(topk)
    zero = jnp.zeros((Q_BLOCK, 1), jnp.int32)
    prefix = jnp.where(count_ge(zero) >= kf, zero, jnp.int32(-2 ** 31))

    def bit_step(t, prefix):
        cand = prefix + lax.shift_left(jnp.int32(1), 30 - t)
        return jnp.where(count_ge(cand) >= kf, cand, prefix)

    prefix = lax.fori_loop(0, 31, bit_step, prefix)

    def bias_chunk(c, carry):
        keep = (key_ref[:, chunk(c)] >= prefix) & (c * kc + kiota <= qpos)
        bias_ref[:, chunk(c)] = jnp.where(keep, 0.0, NEG)
        return carry

    lax.fori_loop(0, nch, bias_chunk, 0)

    scale = HEAD_DIM ** -0.5
    for h in range(A_HEADS):
        qh = q_ref[:, h * HEAD_DIM:(h + 1) * HEAD_DIM].astype(BF16)
        ql_ref[h * Q_BLOCK:(h + 1) * Q_BLOCK, :] = (_dot_nt(qh, wuk_ref[h]) * scale).astype(BF16)
    m_ref[...] = jnp.full(m_ref.shape, -jnp.inf, F32)
    l_ref[...] = jnp.zeros_like(l_ref)
    acc_ref[...] = jnp.zeros_like(acc_ref)

    def attend(c, carry):
        kv = ckvn_ref[chunk(c), :]
        s = _dot_nt(ql_ref[...], kv)
        s = (s.reshape(A_HEADS, Q_BLOCK, kc) + bias_ref[:, chunk(c)][None]).reshape(A_HEADS * Q_BLOCK, kc)
        m_old = m_ref[...]
        m_new = jnp.maximum(m_old, jnp.max(s, axis=-1, keepdims=True))
        p = jnp.exp(s - m_new)
        alpha = jnp.exp(m_old - m_new)
        l_ref[...] = alpha * l_ref[...] + jnp.sum(p, axis=-1, keepdims=True)
        acc_ref[...] = alpha * acc_ref[...] + _dot(p.astype(BF16), kv)
        m_ref[...] = m_new
        return carry

    lax.fori_loop(0, nch, attend, 0)
    ol = (acc_ref[...] / l_ref[...]).astype(BF16)
    for h in range(A_HEADS):
        o_ref[:, h * HEAD_DIM:(h + 1) * HEAD_DIM] = _dot(ol[h * Q_BLOCK:(h + 1) * Q_BLOCK], wuv_ref[h])


def _dsa(y0, kv_gain, w_uk, w_uv):
    bsz, seq, _ = y0.shape
    topk = min(DSA_TOPK_MAX, seq // 4)
    nb = seq // Q_BLOCK
    kern = functools.partial(_dsa_kernel, seq=seq, topk=topk)
    return pl.pallas_call(
        kern,
        grid=(bsz, nb),
        in_specs=[
            pl.BlockSpec((None, Q_BLOCK, 1024), lambda b, i: (b, i, L0_Q // 1024)),
            pl.BlockSpec((None, Q_BLOCK, 1024), lambda b, i: (b, i, L0_IQ // 1024)),
            pl.BlockSpec((None, Q_BLOCK, LANES), lambda b, i: (b, i, L0_IW // LANES)),
            pl.BlockSpec((None, seq, A_KV_RANK), lambda b, i: (b, 0, L0_CKV // A_KV_RANK)),
            pl.BlockSpec((None, seq, LANES), lambda b, i: (b, 0, L0_IK // LANES)),
            pl.BlockSpec((1, A_KV_RANK), lambda b, i: (0, 0)),
            pl.BlockSpec((A_HEADS, A_KV_RANK, HEAD_DIM), lambda b, i: (0, 0, 0)),
            pl.BlockSpec((A_HEADS, A_KV_RANK, HEAD_DIM), lambda b, i: (0, 0, 0)),
        ],
        out_specs=pl.BlockSpec((None, Q_BLOCK, A_HEADS * HEAD_DIM), lambda b, i: (b, i, 0)),
        out_shape=jax.ShapeDtypeStruct((bsz, seq, A_HEADS * HEAD_DIM), F32),
        scratch_shapes=[pltpu.VMEM((seq, A_KV_RANK), BF16), pltpu.VMEM((seq, LANES), BF16),
                        pltpu.VMEM((seq, LANES), BF16), pltpu.VMEM((Q_BLOCK, seq), jnp.int32),
                        pltpu.VMEM((Q_BLOCK, seq), F32),
                        pltpu.VMEM((A_HEADS * Q_BLOCK, A_KV_RANK), BF16),
                        pltpu.VMEM((A_HEADS * Q_BLOCK, 1), F32), pltpu.VMEM((A_HEADS * Q_BLOCK, 1), F32),
                        pltpu.VMEM((A_HEADS * Q_BLOCK, A_KV_RANK), F32)],
        compiler_params=_cp("arbitrary", "arbitrary"),
        name="dsa",
    )(y0, y0, y0, y0, y0, kv_gain.reshape(1, -1), w_uk.astype(BF16), w_uv.astype(BF16))


def _nsa_cmp_kernel(x_ref, pos_ref, w1_ref, w2_ref, o_ref, *, nblk):
    half = NSA_CMP_LEN // 2
    a = jnp.zeros((nblk, HEAD_DIM), F32)
    b = jnp.zeros((nblk, HEAD_DIM), F32)
    for l in range(half):
        rows = x_ref[pl.ds(l, nblk, stride=NSA_CMP_STRIDE), :]
        a = a + _dot((rows + pos_ref[l:l + 1, :]).astype(BF16), w1_ref[l].astype(BF16))
        b = b + _dot((rows + pos_ref[l + half:l + half + 1, :]).astype(BF16),
                     w1_ref[l + half].astype(BF16))
    h = jax.nn.gelu(a + pltpu.roll(b, nblk - 1, axis=0))
    o_ref[...] = _dot(h.astype(BF16), w2_ref[...].astype(BF16))


def _nsa_compress(y0, pos, w1, w2):
    bsz, seq, _ = y0.shape
    nblk = seq // NSA_CMP_STRIDE
    kern = functools.partial(_nsa_cmp_kernel, nblk=nblk)
    return pl.pallas_call(
        kern,
        grid=(bsz, 2, B_KV_GROUPS),
        in_specs=[
            pl.BlockSpec((None, seq, HEAD_DIM), lambda b, t, g: (b, 0, L0_KV // HEAD_DIM + 2 * t + g)),
            pl.BlockSpec((None, NSA_CMP_LEN, HEAD_DIM), lambda b, t, g: (t, 0, 0)),
            pl.BlockSpec((None, NSA_CMP_LEN, HEAD_DIM, HEAD_DIM), lambda b, t, g: (t, 0, 0, 0)),
            pl.BlockSpec((None, HEAD_DIM, HEAD_DIM), lambda b, t, g: (t, 0, 0)),
        ],
        out_specs=pl.BlockSpec((None, None, None, nblk, HEAD_DIM), lambda b, t, g: (b, t, g, 0, 0)),
        out_shape=jax.ShapeDtypeStruct((bsz, 2, B_KV_GROUPS, nblk, HEAD_DIM), F32),
        compiler_params=_cp("arbitrary", "arbitrary", "arbitrary"),
        name="nsa_compress",
    )(y0, pos, w1, w2)


def _nsa_kernel(q_ref, gate_ref, kc_ref, vc_ref, ks_ref, vs_ref, kw_ref, vw_ref, ovl_ref, exp_ref,
                o_ref, ksb_ref, vsb_ref, kwb_ref, vwb_ref, bias_ref, *, seq, nsel):
    i = pl.program_id(2)
    hpg = B_HEADS // B_KV_GROUPS
    rows = hpg * Q_BLOCK
    ncmp = kc_ref.shape[0]
    wlen = NSA_WINDOW + Q_BLOCK

    @pl.when(i == 0)
    def _():
        ksb_ref[...] = ks_ref[...].astype(BF16)
        vsb_ref[...] = vs_ref[...].astype(BF16)
        kwb_ref[...] = kw_ref[...].astype(BF16)
        vwb_ref[...] = vw_ref[...].astype(BF16)

    start = i * Q_BLOCK
    scale = HEAD_DIM ** -0.5
    q4 = jnp.concatenate([q_ref[:, h * HEAD_DIM:(h + 1) * HEAD_DIM] for h in range(hpg)], axis=0)
    q4 = (q4 * scale).astype(BF16)
    qrow = lax.broadcasted_iota(jnp.int32, (Q_BLOCK, 1), 0) + start
    qrow4 = jnp.concatenate([qrow] * hpg, axis=0)

    s_c = _dot_nt(q4, kc_ref[...].astype(BF16))
    cend = lax.broadcasted_iota(jnp.int32, (1, ncmp), 1) * NSA_CMP_STRIDE + (NSA_CMP_LEN - 1)
    vis_c = cend <= qrow4
    p_c = _softmax_rows(jnp.where(vis_c, s_c, NEG))
    p_c = jnp.where(qrow4 >= NSA_CMP_LEN - 1, p_c, 0.0)
    o_c = _dot(p_c.astype(BF16), vc_ref[...].astype(BF16))

    psum = p_c[0:Q_BLOCK]
    for h in range(1, hpg):
        psum = psum + p_c[h * Q_BLOCK:(h + 1) * Q_BLOCK]
    imp = jnp.dot(psum, ovl_ref[...], preferred_element_type=F32, precision=lax.Precision.HIGHEST)
    nslc = seq // NSA_SLC_LEN
    jj = lax.broadcasted_iota(jnp.int32, (1, LANES), 1)
    cur = qrow // NSA_SLC_LEN
    forced = (jj == 0) | (jj == cur) | (jj == cur - 1)
    imp = jnp.where(forced, FORCE, imp)
    imp = jnp.where(jj <= cur, imp, NEG)
    imp = jnp.where(jj < nslc, imp, -3e38)
    rank = jnp.zeros((Q_BLOCK, LANES), F32)
    for t in range(nslc):
        col = imp[:, t:t + 1]
        later = jnp.where(jj > t, 1.0, 0.0)
        rank = rank + jnp.where(col > imp, 1.0, jnp.where(col == imp, later, 0.0))
    selb = jnp.where(rank < float(nsel), 1.0, 0.0).astype(BF16)
    keysel = _dot(selb, exp_ref[...])
    kpos = lax.broadcasted_iota(jnp.int32, (1, seq), 1)
    bias_ref[...] = jnp.where((keysel > 0.5) & (kpos <= qrow), 0.0, NEG)

    ws = pl.multiple_of(jnp.maximum(start - NSA_WINDOW, 0), Q_BLOCK)
    kposw = ws + lax.broadcasted_iota(jnp.int32, (1, wlen), 1)
    vis_w = (kposw <= qrow4) & (kposw > qrow4 - NSA_WINDOW)
    s_w = _dot_nt(q4, kwb_ref[pl.ds(ws, wlen), :])
    p_w = _softmax_rows(jnp.where(vis_w, s_w, NEG))
    o_w = _dot(p_w.astype(BF16), vwb_ref[pl.ds(ws, wlen), :])

    gate = jax.nn.sigmoid(gate_ref[...])
    for h in range(hpg):
        r0, r1 = h * Q_BLOCK, (h + 1) * Q_BLOCK
        s_s = _dot_nt(q4[r0:r1], ksb_ref[...]) + bias_ref[...]
        p_s = _softmax_rows(s_s)
        o_s = _dot(p_s.astype(BF16), vsb_ref[...])
        o = (gate[:, 3 * h:3 * h + 1] * o_c[r0:r1] + gate[:, 3 * h + 1:3 * h + 2] * o_s
             + gate[:, 3 * h + 2:3 * h + 3] * o_w[r0:r1])
        o_ref[:, h * HEAD_DIM:(h + 1) * HEAD_DIM] = o


def _nsa(y0, kvc):
    bsz, seq, _ = y0.shape
    nb = seq // Q_BLOCK
    nslc = seq // NSA_SLC_LEN
    nsel = min(NSA_SLC_N, nslc)
    ncmp = seq // NSA_CMP_STRIDE
    hpg = B_HEADS // B_KV_GROUPS
    cs = np.arange(ncmp)[:, None] * NSA_CMP_STRIDE
    ss = np.arange(LANES)[None, :] * NSA_SLC_LEN
    ovl = ((cs < ss + NSA_SLC_LEN) & (cs + NSA_CMP_LEN > ss) & (np.arange(ncmp)[:, None] < ncmp - 1)
           & (np.arange(LANES)[None, :] < nslc)).astype(np.float32)
    expand = (np.arange(seq)[None, :] // NSA_SLC_LEN == np.arange(LANES)[:, None]).astype(np.float32)
    kvb = L0_KV // HEAD_DIM
    full = lambda off: pl.BlockSpec((None, seq, HEAD_DIM), lambda b, g, i, off=off: (b, 0, kvb + off + g))
    kern = functools.partial(_nsa_kernel, seq=seq, nsel=nsel)
    return pl.pallas_call(
        kern,
        grid=(bsz, B_KV_GROUPS, nb),
        in_specs=[
            pl.BlockSpec((None, Q_BLOCK, hpg * HEAD_DIM), lambda b, g, i: (b, i, L0_BQ // (hpg * HEAD_DIM) + g)),
            pl.BlockSpec((None, Q_BLOCK, LANES), lambda b, g, i: (b, i, L0_GATE // LANES + g)),
            pl.BlockSpec((None, None, None, ncmp, HEAD_DIM), lambda b, g, i: (b, 0, g, 0, 0)),
            pl.BlockSpec((None, None, None, ncmp, HEAD_DIM), lambda b, g, i: (b, 1, g, 0, 0)),
            full(4), full(6), full(8), full(10),
            pl.BlockSpec((ncmp, LANES), lambda b, g, i: (0, 0)),
            pl.BlockSpec((LANES, seq), lambda b, g, i: (0, 0)),
        ],
        out_specs=pl.BlockSpec((None, Q_BLOCK, hpg * HEAD_DIM), lambda b, g, i: (b, i, g)),
        out_shape=jax.ShapeDtypeStruct((bsz, seq, B_HEADS * HEAD_DIM), F32),
        scratch_shapes=[pltpu.VMEM((seq, HEAD_DIM), BF16)] * 4 + [pltpu.VMEM((Q_BLOCK, seq), F32)],
        compiler_params=_cp("arbitrary", "arbitrary", "arbitrary"),
        name="nsa",
    )(y0, y0, kvc, kvc, y0, y0, y0, y0, jnp.asarray(ovl), jnp.asarray(expand, dtype=BF16))


def _out_ln_kernel(a_ref, b_ref, wa_ref, wb_ref, x_ref, g_ref, beta_ref, o_ref):
    acc = _dot(a_ref[...].astype(BF16), wa_ref[...]) + _dot(b_ref[...].astype(BF16), wb_ref[...])
    o_ref[...] = _layer_norm(ALPHA * x_ref[...] + acc, g_ref[...], beta_ref[...])


def _out_ln(a, b, w_out, x, g, beta, tm=256):
    t, ka = a.shape
    kb = b.shape[1]
    d = x.shape[1]
    wa = w_out[:ka].astype(BF16)
    wb = w_out[ka:].astype(BF16)
    return pl.pallas_call(
        _out_ln_kernel,
        grid=(t // tm,),
        in_specs=[pl.BlockSpec((tm, ka), lambda i: (i, 0)), pl.BlockSpec((tm, kb), lambda i: (i, 0)),
                  pl.BlockSpec((ka, d), lambda i: (0, 0)), pl.BlockSpec((kb, d), lambda i: (0, 0)),
                  pl.BlockSpec((tm, d), lambda i: (i, 0)),
                  pl.BlockSpec((1, d), lambda i: (0, 0)), pl.BlockSpec((1, d), lambda i: (0, 0))],
        out_specs=pl.BlockSpec((tm, d), lambda i: (i, 0)),
        out_shape=jax.ShapeDtypeStruct((t, d), F32),
        compiler_params=_cp("arbitrary"),
        name="out_proj_ln",
    )(a, b, wa, wb, x, g.reshape(1, -1), beta.reshape(1, -1))


BIG = 1e30


def _top_values(s, n):
    dist, cum = [], []
    cur = s
    total = jnp.zeros((1, s.shape[1]), F32)
    for _ in range(n):
        m = jnp.max(cur, axis=0, keepdims=True)
        eq = cur == m
        total = total + jnp.sum(jnp.where(eq, 1.0, 0.0), axis=0, keepdims=True)
        dist.append(m)
        cum.append(total)
        cur = jnp.where(eq, -BIG, cur)
    vals = []
    for k in range(1, n + 1):
        v = dist[n - 1]
        for m in range(n - 2, -1, -1):
            v = jnp.where(cum[m] >= float(k), dist[m], v)
        vals.append(v)
    return vals


def _pack_pair(v):
    b = lax.bitcast_convert_type(v.astype(BF16).astype(F32), jnp.int32)
    return b | lax.shift_right_logical(b, 16)


def _peer_stats_kernel(x_ref, wqt_ref, keys_ref, r1_ref, c1_ref, rk2_ref, e2_ref, qt_ref):
    qt_ref[...] = _dot_nt(wqt_ref[...], x_ref[...].astype(BF16))
    half = PEER_DKEY // 2
    kh = PEER_TOPK_HALF
    rows = lax.broadcasted_iota(jnp.int32, (SUBLANES, 1), 0)
    for h in range(PEER_HEADS):
        r0 = h * PEER_DKEY
        s1 = _dot(keys_ref[0], qt_ref[r0:r0 + half, :].astype(BF16))
        s2 = _dot(keys_ref[1], qt_ref[r0 + half:r0 + PEER_DKEY, :].astype(BF16))
        v1 = _top_values(s1, kh)
        v2 = _top_values(s2, kh)
        sv1 = jnp.concatenate(v1, axis=0)
        sv2 = jnp.concatenate(v2, axis=0)
        tiles = [v1[0] + sv2, v1[1] + sv2[0:SUBLANES]]
        for a in range(2, SUBLANES):
            nb = (PEER_TOPK + 1) // (a + 1)
            tiles.append(jnp.where(rows < nb, v1[a] + sv2[0:SUBLANES], -BIG))
        tiles.append(sv1[SUBLANES:] + v2[0])
        cand = jnp.concatenate(tiles, axis=0)
        cv = _top_values(cand, PEER_TOPK + 1)
        top, kth, nxt = cv[0], cv[PEER_TOPK - 1], cv[PEER_TOPK]
        z = jnp.sum(jnp.where(cand >= kth, jnp.exp(cand - top), 0.0), axis=0, keepdims=True)
        thr = 0.5 * (kth + nxt)
        act1 = s1 >= v1[kh - 1]
        act2 = s2 >= v2[kh - 1]
        need = thr - s1
        r1 = jnp.zeros_like(s1)
        rk2 = jnp.zeros_like(s2)
        for k in range(kh):
            r1 = r1 + jnp.where(v2[k] >= need, 1.0, 0.0)
            rk2 = rk2 + jnp.where(v2[k] > s2, 1.0, 0.0)
        r1_ref[h] = _pack_pair(jnp.where(act1, r1, 0.0))
        c1_ref[h] = _pack_pair(jnp.where(act1, jnp.exp(s1 - v1[0]) / z, 0.0))
        rk2_ref[h] = rk2.astype(BF16)
        e2_ref[h] = jnp.where(act2, jnp.exp(s2 - v2[0]), 0.0).astype(BF16)


def _peer_stats(x, w_q, sub_keys, tn=512):
    t, d = x.shape
    wqt = w_q.T.astype(BF16)
    shp = lambda dt: jax.ShapeDtypeStruct((PEER_HEADS, PEER_NKEYS, t), dt)
    spec = pl.BlockSpec((PEER_HEADS, PEER_NKEYS, tn), lambda i: (0, 0, i))
    return pl.pallas_call(
        _peer_stats_kernel,
        grid=(t // tn,),
        in_specs=[pl.BlockSpec((tn, d), lambda i: (i, 0)),
                  pl.BlockSpec((PEER_HEADS * PEER_DKEY, d), lambda i: (0, 0)),
                  pl.BlockSpec((2, PEER_NKEYS, PEER_DKEY // 2), lambda i: (0, 0, 0))],
        out_specs=[spec] * 4,
        out_shape=[shp(jnp.int32), shp(jnp.int32), shp(BF16), shp(BF16)],
        scratch_shapes=[pltpu.VMEM((PEER_HEADS * PEER_DKEY, tn), F32)],
        compiler_params=_cp("arbitrary"),
        name="peer_stats",
    )(x, wqt, sub_keys.astype(BF16))


def _peer_kernel(x_ref, u_ref, vt_ref, r1_ref, c1_ref, rk2_ref, e2_ref, g_ref, beta_ref, o_ref,
                 xb_ref, acc_ref, w_ref, *, te):
    j = pl.program_id(1)
    tn = xb_ref.shape[0]
    pack = 2 * SUBLANES
    nsub = te // PEER_NKEYS

    @pl.when(j == 0)
    def _():
        xb_ref[...] = x_ref[...].astype(BF16)
        acc_ref[...] = jnp.zeros_like(acc_ref)

    def row_bf16(ref, h, idx):
        w = jnp.broadcast_to(ref[h, pl.ds(idx, 1), :], (SUBLANES, tn))
        return pltpu.repeat(pltpu.bitcast(w, BF16), PEER_NKEYS // pack, axis=0)

    act = jax.nn.gelu(_dot_nt(u_ref[...], xb_ref[...]).astype(BF16))
    for ii in range(nsub):
        idx = j * nsub + ii
        gate = jnp.zeros((PEER_NKEYS, tn), BF16)
        for h in range(PEER_HEADS):
            sel = rk2_ref[h] < row_bf16(r1_ref, h, idx)
            gate = gate + jnp.where(sel, e2_ref[h], jnp.zeros((), BF16)) * row_bf16(c1_ref, h, idx)
        sl = slice(ii * PEER_NKEYS, (ii + 1) * PEER_NKEYS)
        w_ref[sl, :] = act[sl, :] * gate
    acc_ref[...] += _dot(vt_ref[...], w_ref[...])

    @pl.when(j == pl.num_programs(1) - 1)
    def _():
        y = ALPHA * x_ref[...] + acc_ref[...].T
        o_ref[...] = _layer_norm(y, g_ref[...], beta_ref[...])


def _peer_ln(x, stats, u_tab, v_tab, g, beta, tn=512, te=512):
    t, d = x.shape
    ne = u_tab.shape[0]
    ub = u_tab.astype(BF16)
    vtb = v_tab.T.astype(BF16)
    sspec = pl.BlockSpec((PEER_HEADS, PEER_NKEYS, tn), lambda i, j: (0, 0, i))
    kern = functools.partial(_peer_kernel, te=te)
    return pl.pallas_call(
        kern,
        grid=(t // tn, ne // te),
        in_specs=[pl.BlockSpec((tn, d), lambda i, j: (i, 0)),
                  pl.BlockSpec((te, d), lambda i, j: (j, 0)),
                  pl.BlockSpec((d, te), lambda i, j: (0, j)),
                  sspec, sspec, sspec, sspec,
                  pl.BlockSpec((1, d), lambda i, j: (0, 0)), pl.BlockSpec((1, d), lambda i, j: (0, 0))],
        out_specs=pl.BlockSpec((tn, d), lambda i, j: (i, 0)),
        out_shape=jax.ShapeDtypeStruct((t, d), F32),
        scratch_shapes=[pltpu.VMEM((tn, d), BF16), pltpu.VMEM((d, tn), F32), pltpu.VMEM((te, tn), BF16)],
        compiler_params=_cp("arbitrary", "arbitrary"),
        name="peer",
    )(x, ub, vtb, *stats, g.reshape(1, -1), beta.reshape(1, -1))


def _shortconv_kernel(cb_ref, cc_ref, ch_ref, ccp_ref, chp_ref, w_ref, o_ref, ext_ref, *, ts):
    i = pl.program_id(1)
    prev = ccp_ref[...] * chp_ref[...]
    ext_ref[0:SUBLANES, :] = jnp.where(i > 0, prev, 0.0)
    ext_ref[SUBLANES:, :] = cc_ref[...] * ch_ref[...]
    acc = jnp.zeros(o_ref.shape, F32)
    for k in range(CONV_W):
        off = SUBLANES - (CONV_W - 1) + k
        acc = acc + w_ref[k:k + 1, :] * ext_ref[off:off + ts, :]
    o_ref[...] = cb_ref[...] * acc


def _shortconv(y1, conv_w, ts=512):
    bsz, seq, _ = y1.shape
    blk = lambda c: pl.BlockSpec((None, ts, CONV_CH), lambda b, i, c=c: (b, i, c))
    halo = lambda c: pl.BlockSpec((None, SUBLANES, CONV_CH),
                                  lambda b, i, c=c: (b, jnp.maximum(i * (ts // SUBLANES) - 1, 0), c))
    kern = functools.partial(_shortconv_kernel, ts=ts)
    return pl.pallas_call(
        kern,
        grid=(bsz, seq // ts),
        in_specs=[blk(0), blk(1), blk(2), halo(1), halo(2),
                  pl.BlockSpec((CONV_W, CONV_CH), lambda b, i: (0, 0))],
        out_specs=pl.BlockSpec((None, ts, CONV_CH), lambda b, i: (b, i, 0)),
        out_shape=jax.ShapeDtypeStruct((bsz, seq, CONV_CH), F32),
        scratch_shapes=[pltpu.VMEM((ts + SUBLANES, CONV_CH), F32)],
        compiler_params=_cp("arbitrary", "arbitrary"),
        name="short_conv",
    )(y1, y1, y1, y1, y1, conv_w)


def _ssdconv_kernel(u_ref, up_ref, w_ref, b_ref, o_ref, ext_ref, *, ts):
    i = pl.program_id(1)
    ext_ref[0:SUBLANES, :] = jnp.where(i > 0, up_ref[...], 0.0)
    ext_ref[SUBLANES:, :] = u_ref[...]
    acc = jnp.zeros(o_ref.shape, F32) + b_ref[...]
    for k in range(SSD_CONV_W):
        off = SUBLANES - (SSD_CONV_W - 1) + k
        acc = acc + w_ref[k:k + 1, :] * ext_ref[off:off + ts, :]
    o_ref[...] = acc * jax.nn.sigmoid(acc)


def _ssdconv(y1, w, bias, ts=512, tc=1024):
    bsz, seq, _ = y1.shape
    nch = w.shape[1]
    c0 = L1_XS // tc
    kern = functools.partial(_ssdconv_kernel, ts=ts)
    return pl.pallas_call(
        kern,
        grid=(bsz, seq // ts, nch // tc),
        in_specs=[pl.BlockSpec((None, ts, tc), lambda b, i, c: (b, i, c0 + c)),
                  pl.BlockSpec((None, SUBLANES, tc),
                               lambda b, i, c: (b, jnp.maximum(i * (ts // SUBLANES) - 1, 0), c0 + c)),
                  pl.BlockSpec((SSD_CONV_W, tc), lambda b, i, c: (0, c)),
                  pl.BlockSpec((1, tc), lambda b, i, c: (0, c))],
        out_specs=pl.BlockSpec((None, ts, tc), lambda b, i, c: (b, i, c)),
        out_shape=jax.ShapeDtypeStruct((bsz, seq, nch), F32),
        scratch_shapes=[pltpu.VMEM((ts + SUBLANES, tc), F32)],
        compiler_params=_cp("arbitrary", "arbitrary", "arbitrary"),
        name="ssd_conv",
    )(y1, y1, w, bias.reshape(1, -1))


def _ssd_kernel(xs_ref, bs_ref, cs_ref, z_ref, dt_ref, dtb_ref, aneg_ref, dskip_ref, ng_ref, o_ref,
                state_ref):
    c = pl.program_id(2)
    lc = SSD_CHUNK
    npair = SSD_REP // 2

    @pl.when(c == 0)
    def _():
        state_ref[...] = jnp.zeros_like(state_ref)

    dt = jax.nn.softplus(dt_ref[...] + dtb_ref[...])
    da = dt * aneg_ref[...]
    ri = lax.broadcasted_iota(jnp.int32, (lc, lc), 0)
    ci = lax.broadcasted_iota(jnp.int32, (lc, lc), 1)
    causal = ci <= ri
    tril = jnp.where(causal, 1.0, 0.0)
    a_cs = jnp.dot(tril, da, preferred_element_type=F32, precision=lax.Precision.HIGHEST)
    a_cs_t = a_cs.T
    a_last = a_cs[lc - 1:lc, :]
    decay_s = jnp.exp(a_last - a_cs)
    decay_o = jnp.exp(a_cs)
    chunk_decay = jnp.exp(a_last)

    bc = bs_ref[...].astype(BF16)
    cc = cs_ref[...].astype(BF16)
    cb = _dot_nt(cc, bc)
    lane = lax.broadcasted_iota(jnp.int32, (lc, LANES), 1)
    lo_half = lane < SSD_HEAD_DIM
    row_lo = lax.broadcasted_iota(jnp.int32, (LANES, 1), 0) < SSD_HEAD_DIM

    hz = []
    ssq = jnp.zeros((lc, 1), F32)
    for pr in range(npair):
        r0, r1 = 2 * pr, 2 * pr + 1
        sl = slice(pr * LANES, (pr + 1) * LANES)
        x = xs_ref[:, sl]
        pick = lambda m: jnp.where(lo_half, m[:, r0:r0 + 1], m[:, r1:r1 + 1])
        xdt = x * pick(dt)
        y = jnp.zeros((lc, LANES), F32)
        for r, keep in ((r0, lo_half), (r1, ~lo_half)):
            seg = a_cs[:, r:r + 1] - a_cs_t[r:r + 1, :]
            m = cb * jnp.exp(jnp.where(causal, seg, NEG))
            y = y + _dot(m.astype(BF16), jnp.where(keep, xdt, 0.0).astype(BF16))
        prev = state_ref[sl, :]
        y = y + _dot_nt(cc, prev.astype(BF16)) * pick(decay_o)
        st = _dot_tn((xdt * pick(decay_s)).astype(BF16), bc)
        cd = jnp.where(row_lo, chunk_decay[:, r0:r0 + 1], chunk_decay[:, r1:r1 + 1])
        state_ref[sl, :] = prev * cd + st
        y = y + jnp.where(lo_half, dskip_ref[:, r0:r0 + 1], dskip_ref[:, r1:r1 + 1]) * x
        z = z_ref[:, sl]
        h = y * (z * jax.nn.sigmoid(z))
        hz.append(h)
        ssq = ssq + jnp.sum(h * h, axis=-1, keepdims=True)
    inv = lax.rsqrt(ssq / (SSD_REP * SSD_HEAD_DIM) + EPS)
    for pr in range(npair):
        sl = slice(pr * LANES, (pr + 1) * LANES)
        o_ref[:, sl] = hz[pr] * inv * ng_ref[:, sl]


def _ssd(y1, xbc, dt_bias_g, a_neg_g, d_skip_g, norm_g):
    bsz, seq, _ = y1.shape
    gw = SSD_REP * SSD_HEAD_DIM
    nchunk = seq // SSD_CHUNK
    vec = pl.BlockSpec((None, 1, LANES), lambda b, g, c: (g, 0, 0))
    return pl.pallas_call(
        _ssd_kernel,
        grid=(bsz, SSD_GROUPS, nchunk),
        in_specs=[
            pl.BlockSpec((None, SSD_CHUNK, gw), lambda b, g, c: (b, c, g)),
            pl.BlockSpec((None, SSD_CHUNK, SSD_STATE), lambda b, g, c: (b, c, SSD_INNER // SSD_STATE + g)),
            pl.BlockSpec((None, SSD_CHUNK, SSD_STATE),
                         lambda b, g, c: (b, c, SSD_INNER // SSD_STATE + SSD_GROUPS + g)),
            pl.BlockSpec((None, SSD_CHUNK, gw), lambda b, g, c: (b, c, L1_Z // gw + g)),
            pl.BlockSpec((None, SSD_CHUNK, LANES), lambda b, g, c: (b, c, L1_DT // LANES + g)),
            vec, vec, vec,
            pl.BlockSpec((1, gw), lambda b, g, c: (0, g)),
        ],
        out_specs=pl.BlockSpec((None, SSD_CHUNK, gw), lambda b, g, c: (b, c, g)),
        out_shape=jax.ShapeDtypeStruct((bsz, seq, SSD_INNER), F32),
        scratch_shapes=[pltpu.VMEM((gw, SSD_STATE), F32)],
        compiler_params=_cp("arbitrary", "arbitrary", "arbitrary"),
        name="ssd",
    )(xbc, xbc, xbc, y1, y1, dt_bias_g, a_neg_g, d_skip_g, norm_g.reshape(1, -1))


def _pad_cols(w, n):
    return jnp.pad(w, ((0, 0), (0, n - w.shape[1])))


def _pack_w_in0(w):
    o = np.cumsum((0, A_HEADS * HEAD_DIM, A_KV_RANK, IDX_HEADS * IDX_DIM, IDX_DIM, IDX_HEADS,
                   B_HEADS * HEAD_DIM) + (B_KV_GROUPS * HEAD_DIM,) * 6 + (B_HEADS * 3,))
    seg = lambda k: w[:, int(o[k]):int(o[k + 1])]
    gates = seg(12)
    ng = gates.shape[1] // B_KV_GROUPS
    parts = [seg(0), seg(2), seg(5)] + [seg(k) for k in range(6, 12)] + [
        seg(1), _pad_cols(seg(3), LANES), _pad_cols(seg(4), LANES)] + [
        _pad_cols(gates[:, g * ng:(g + 1) * ng], LANES) for g in range(B_KV_GROUPS)]
    return jnp.concatenate(parts, axis=1).astype(BF16)


def _pack_w_in1(w):
    dt = w[:, L1_DT:]
    parts = [w[:, :L1_DT]] + [_pad_cols(dt[:, g * SSD_REP:(g + 1) * SSD_REP], LANES) for g in range(SSD_GROUPS)]
    return jnp.concatenate(parts, axis=1).astype(BF16)


def _per_group(v):
    return jnp.pad(v.reshape(SSD_GROUPS, 1, SSD_REP), ((0, 0), (0, 0), (0, LANES - SSD_REP)))


def _peer_block(x, wq, keys, u, v, g, b):
    stats = _peer_stats(x, wq, keys)
    return _peer_ln(x, stats, u, v, g, b)


def _even_layer(x, bsz, seq, w_in, kv_gain, w_uk, w_uv, cmpk_pos, cmpk_w1, cmpk_w2, cmpv_pos, cmpv_w1, cmpv_w2,
                w_out, ln1_g, ln1_b, peer_wq, peer_keys, peer_u, peer_v, ln2_g, ln2_b):
    t = bsz * seq
    y0 = _matmul(x, _pack_w_in0(w_in)).reshape(bsz, seq, NP0)
    o_a = _dsa(y0, kv_gain, w_uk, w_uv)
    kvc = _nsa_compress(y0, jnp.stack([cmpk_pos, cmpv_pos]), jnp.stack([cmpk_w1, cmpv_w1]),
                        jnp.stack([cmpk_w2, cmpv_w2]))
    o_b = _nsa(y0, kvc)
    x = _out_ln(o_a.reshape(t, -1), o_b.reshape(t, -1), w_out, x, ln1_g, ln1_b)
    return _peer_block(x, peer_wq, peer_keys, peer_u, peer_v, ln2_g, ln2_b)


def _odd_layer(x, bsz, seq, w_in, conv_w, ssd_conv_w, ssd_conv_b, dt_bias, a_log, d_skip, norm_g,
               w_out, ln1_g, ln1_b, peer_wq, peer_keys, peer_u, peer_v, ln2_g, ln2_b):
    t = bsz * seq
    y1 = _matmul(x, _pack_w_in1(w_in)).reshape(bsz, seq, NP1)
    y_c = _shortconv(y1, conv_w)
    xbc = _ssdconv(y1, ssd_conv_w, ssd_conv_b)
    y_d = _ssd(y1, xbc, _per_group(dt_bias), _per_group(-jnp.exp(a_log)), _per_group(d_skip), norm_g)
    x = _out_ln(y_c.reshape(t, -1), y_d.reshape(t, -1), w_out, x, ln1_g, ln1_b)
    return _peer_block(x, peer_wq, peer_keys, peer_u, peer_v, ln2_g, ln2_b)


def kernel(x, l0_w_in, l0_kv_gain, l0_w_uk, l0_w_uv, l0_cmpk_pos, l0_cmpk_w1, l0_cmpk_w2, l0_cmpv_pos, l0_cmpv_w1, l0_cmpv_w2, l0_w_out, l0_ln1_g, l0_ln1_b, l0_peer_wq, l0_peer_keys, l0_peer_u, l0_peer_v, l0_ln2_g, l0_ln2_b, l1_w_in, l1_conv_w, l1_ssd_conv_w, l1_ssd_conv_b, l1_dt_bias, l1_a_log, l1_d_skip, l1_norm_g, l1_w_out, l1_ln1_g, l1_ln1_b, l1_peer_wq, l1_peer_keys, l1_peer_u, l1_peer_v, l1_ln2_g, l1_ln2_b):
    bsz, seq, d = x.shape
    h = x.reshape(bsz * seq, d)
    h = _even_layer(h, bsz, seq, l0_w_in, l0_kv_gain, l0_w_uk, l0_w_uv, l0_cmpk_pos, l0_cmpk_w1, l0_cmpk_w2,
                    l0_cmpv_pos, l0_cmpv_w1, l0_cmpv_w2, l0_w_out, l0_ln1_g, l0_ln1_b,
                    l0_peer_wq, l0_peer_keys, l0_peer_u, l0_peer_v, l0_ln2_g, l0_ln2_b)
    h = _odd_layer(h, bsz, seq, l1_w_in, l1_conv_w, l1_ssd_conv_w, l1_ssd_conv_b, l1_dt_bias, l1_a_log,
                   l1_d_skip, l1_norm_g, l1_w_out, l1_ln1_g, l1_ln1_b,
                   l1_peer_wq, l1_peer_keys, l1_peer_u, l1_peer_v, l1_ln2_g, l1_ln2_b)
    return h.reshape(bsz, seq, d)
```

```python
import functools

import numpy as np
import jax
import jax.numpy as jnp
from jax import lax
from jax.experimental import pallas as pl
from jax.experimental.pallas import tpu as pltpu

F32 = jnp.float32
BF16 = jnp.bfloat16

D_MODEL = 2048
HEAD_DIM = 128
Q_BLOCK = 128
NEG = -1e30
FORCE = 1e30
EPS = 1e-5

A_HEADS = 8
A_KV_RANK = 256
IDX_HEADS = 16
IDX_DIM = 64
DSA_TOPK_MAX = 256
DSA_KEY_CHUNK = 512
DSA_SUM_ROWS = 16

B_HEADS = 8
B_KV_GROUPS = 2
NSA_CMP_LEN = 32
NSA_CMP_STRIDE = 16
NSA_SLC_LEN = 64
NSA_SLC_N = 16
NSA_WINDOW = 512
NSA_KEY_CHUNK = 512
NSA_SUM_ROWS = 16

CONV_CH = 1024
CONV_W = 3

SSD_HEAD_DIM = 64
SSD_HEADS = 48
SSD_INNER = SSD_HEADS * SSD_HEAD_DIM
SSD_GROUPS = 8
SSD_STATE = 128
SSD_CONV_W = 4
SSD_CHUNK = 128
SSD_REP = SSD_HEADS // SSD_GROUPS

PEER_HEADS = 8
PEER_NKEYS = 128
PEER_N = PEER_NKEYS * PEER_NKEYS
PEER_DKEY = 256
PEER_TOPK_HALF = 16
PEER_TOPK = 16

DEPTH = 2
ALPHA = (2 * DEPTH) ** 0.25

LANES = 128
SUBLANES = 8
VMEM_LIMIT = 56 * 1024 * 1024

L0_Q, L0_IQ, L0_BQ, L0_KV, L0_CKV, L0_IK, L0_IW, L0_GATE = 0, 1024, 2048, 3072, 4608, 4864, 4992, 5120
NP0 = 5376
L1_Z, L1_XS, L1_BS, L1_CS, L1_DT = 3072, 6144, 9216, 10240, 11264
NP1 = L1_DT + SSD_GROUPS * LANES


def _cp(*sem):
    return pltpu.CompilerParams(dimension_semantics=sem, vmem_limit_bytes=VMEM_LIMIT)


def _dot(a, b):
    return jnp.dot(a, b, preferred_element_type=F32)


def _dot_nt(a, b):
    return lax.dot_general(a, b, (((1,), (1,)), ((), ())), preferred_element_type=F32)


def _dot_tn(a, b):
    return lax.dot_general(a, b, (((0,), (0,)), ((), ())), preferred_element_type=F32)


def _layer_norm(v, g, b):
    mu = jnp.mean(v, axis=-1, keepdims=True)
    d = v - mu
    var = jnp.mean(d * d, axis=-1, keepdims=True)
    return d * lax.rsqrt(var + EPS) * g + b


def _mm_kernel(x_ref, w_ref, o_ref, xb_ref):
    @pl.when(pl.program_id(1) == 0)
    def _():
        xb_ref[...] = x_ref[...].astype(BF16)

    o_ref[...] = _dot(xb_ref[...], w_ref[...])


def _matmul(x, w, tm=512, tn=768):
    t, k = x.shape
    n = w.shape[1]
    return pl.pallas_call(
        _mm_kernel,
        grid=(t // tm, n // tn),
        in_specs=[pl.BlockSpec((tm, k), lambda i, j: (i, 0)),
                  pl.BlockSpec((k, tn), lambda i, j: (0, j))],
        out_specs=pl.BlockSpec((tm, tn), lambda i, j: (i, j)),
        out_shape=jax.ShapeDtypeStruct((t, n), F32),
        scratch_shapes=[pltpu.VMEM((tm, k), BF16)],
        compiler_params=_cp("arbitrary", "arbitrary"),
        name="in_proj",
    )(x, w)


def _dsa_kernel(q_ref, iq_ref, iw_ref, ckv_ref, ik_ref, gain_ref, wuk_ref, wuv_ref, o_ref,
                ckvn_ref, ckvt_ref, iklo_ref, ikhi_ref, key_ref, bias_ref, qlt_ref, m_ref, acc_ref,
                *, seq, topk):
    i = pl.program_id(1)

    @pl.when(i == 0)
    def _():
        c = ckv_ref[...]
        ms = jnp.mean(c * c, axis=-1, keepdims=True)
        cn = c * lax.rsqrt(ms + EPS) * gain_ref[...]
        ckvn_ref[...] = cn.astype(BF16)
        ckvt_ref[0:A_KV_RANK, :] = cn.T.astype(BF16)
        ckvt_ref[A_KV_RANK:, :] = jnp.ones((DSA_SUM_ROWS, seq), BF16)
        ik = ik_ref[...]
        iklo_ref[...] = ik.astype(BF16)
        ikhi_ref[...] = pltpu.roll(ik, IDX_DIM, axis=1).astype(BF16)

    kc = DSA_KEY_CHUNK
    nch = (i * Q_BLOCK + Q_BLOCK + kc - 1) // kc
    qpos = i * Q_BLOCK + lax.broadcasted_iota(jnp.int32, (kc, Q_BLOCK), 1)
    kiota = lax.broadcasted_iota(jnp.int32, (kc, Q_BLOCK), 0)

    def chunk(c):
        return pl.ds(pl.multiple_of(c * kc, kc), kc)

    iwt = iw_ref[...].T * ((IDX_DIM * IDX_HEADS) ** -0.5)
    npair = IDX_HEADS // 2
    iq2 = [jnp.concatenate([iq_ref[:, p * LANES:(p + 1) * LANES], iq_ref[:, (p + 1) * LANES:(p + 2) * LANES]],
                           axis=0).astype(BF16) for p in range(0, npair, 2)]

    def index_chunk(c, carry):
        lo = iklo_ref[chunk(c), :]
        hi = ikhi_ref[chunk(c), :]
        score = jnp.zeros((kc, Q_BLOCK), F32)
        for t, iqt in enumerate(iq2):
            d_lo = jnp.maximum(_dot_nt(lo, iqt), 0.0)
            d_hi = jnp.maximum(_dot_nt(hi, iqt), 0.0)
            for u in range(2):
                h0 = 4 * t + 2 * u
                score = score + d_lo[:, u * Q_BLOCK:(u + 1) * Q_BLOCK] * iwt[h0:h0 + 1, :]
                score = score + d_hi[:, u * Q_BLOCK:(u + 1) * Q_BLOCK] * iwt[h0 + 1:h0 + 2, :]
        score = jnp.where(c * kc + kiota <= qpos, score, NEG)
        bits = lax.bitcast_convert_type(score, jnp.int32)
        key_ref[chunk(c), :] = jnp.where(bits < 0, bits ^ jnp.int32(0x7FFFFFFF), bits)
        return carry

    lax.fori_loop(0, nch, index_chunk, 0)

    def count_ge(cand):
        def body(c, acc):
            ge = jnp.where(key_ref[chunk(c), :] >= cand, 1.0, 0.0)
            while ge.shape[0] > SUBLANES:
                half = ge.shape[0] // 2
                ge = ge[:half] + ge[half:]
            return acc + ge
        acc = lax.fori_loop(0, nch, body, jnp.zeros((SUBLANES, Q_BLOCK), F32))
        return jnp.sum(acc, axis=0, keepdims=True)

    kf = float(topk)
    zero = jnp.zeros((1, Q_BLOCK), jnp.int32)
    prefix = jnp.where(count_ge(zero) >= kf, zero, jnp.int32(-2 ** 31))

    def bit_step(t, prefix):
        cand = prefix + lax.shift_left(jnp.int32(1), 30 - t)
        return jnp.where(count_ge(cand) >= kf, cand, prefix)

    prefix = lax.fori_loop(0, 31, bit_step, prefix)

    def bias_chunk(c, carry):
        keep = (key_ref[chunk(c), :] >= prefix) & (c * kc + kiota <= qpos)
        bias_ref[chunk(c), :] = jnp.where(keep, 0.0, NEG).astype(BF16)
        return carry

    lax.fori_loop(0, nch, bias_chunk, 0)

    scale = HEAD_DIM ** -0.5
    for h in range(A_HEADS):
        qh = q_ref[:, h * HEAD_DIM:(h + 1) * HEAD_DIM].astype(BF16)
        qlt_ref[:, h * Q_BLOCK:(h + 1) * Q_BLOCK] = (_dot_nt(wuk_ref[h], qh) * scale).astype(BF16)
    m_ref[...] = jnp.full(m_ref.shape, -jnp.inf, F32)
    acc_ref[...] = jnp.zeros_like(acc_ref)

    def attend(c, carry):
        s = _dot(ckvn_ref[chunk(c), :], qlt_ref[...]).astype(BF16)
        s = s + jnp.concatenate([bias_ref[chunk(c), :]] * A_HEADS, axis=1)
        m_old = m_ref[...]
        m_new = jnp.maximum(m_old, jnp.max(s, axis=0, keepdims=True).astype(F32))
        p = jnp.exp(s - m_new.astype(BF16))
        alpha = jnp.exp(m_old - m_new)
        acc_ref[...] = alpha * acc_ref[...] + _dot(ckvt_ref[:, chunk(c)], p)
        m_ref[...] = m_new
        return carry

    lax.fori_loop(0, nch, attend, 0)
    olt = (acc_ref[0:A_KV_RANK, :] / acc_ref[A_KV_RANK:A_KV_RANK + 1, :]).astype(BF16)
    for h in range(A_HEADS):
        o_ref[:, h * HEAD_DIM:(h + 1) * HEAD_DIM] = _dot_tn(olt[:, h * Q_BLOCK:(h + 1) * Q_BLOCK], wuv_ref[h])


def _dsa(y0, kv_gain, w_uk, w_uv):
    bsz, seq, _ = y0.shape
    topk = min(DSA_TOPK_MAX, seq // 4)
    nb = seq // Q_BLOCK
    nh = A_HEADS * Q_BLOCK
    kern = functools.partial(_dsa_kernel, seq=seq, topk=topk)
    return pl.pallas_call(
        kern,
        grid=(bsz, nb),
        in_specs=[
            pl.BlockSpec((None, Q_BLOCK, 1024), lambda b, i: (b, i, L0_Q // 1024)),
            pl.BlockSpec((None, Q_BLOCK, 1024), lambda b, i: (b, i, L0_IQ // 1024)),
            pl.BlockSpec((None, Q_BLOCK, LANES), lambda b, i: (b, i, L0_IW // LANES)),
            pl.BlockSpec((None, seq, A_KV_RANK), lambda b, i: (b, 0, L0_CKV // A_KV_RANK)),
            pl.BlockSpec((None, seq, LANES), lambda b, i: (b, 0, L0_IK // LANES)),
            pl.BlockSpec((1, A_KV_RANK), lambda b, i: (0, 0)),
            pl.BlockSpec((A_HEADS, A_KV_RANK, HEAD_DIM), lambda b, i: (0, 0, 0)),
            pl.BlockSpec((A_HEADS, A_KV_RANK, HEAD_DIM), lambda b, i: (0, 0, 0)),
        ],
        out_specs=pl.BlockSpec((None, Q_BLOCK, A_HEADS * HEAD_DIM), lambda b, i: (b, i, 0)),
        out_shape=jax.ShapeDtypeStruct((bsz, seq, A_HEADS * HEAD_DIM), F32),
        scratch_shapes=[pltpu.VMEM((seq, A_KV_RANK), BF16), pltpu.VMEM((A_KV_RANK + DSA_SUM_ROWS, seq), BF16),
                        pltpu.VMEM((seq, LANES), BF16), pltpu.VMEM((seq, LANES), BF16),
                        pltpu.VMEM((seq, Q_BLOCK), jnp.int32), pltpu.VMEM((seq, Q_BLOCK), BF16),
                        pltpu.VMEM((A_KV_RANK, nh), BF16),
                        pltpu.VMEM((1, nh), F32),
                        pltpu.VMEM((A_KV_RANK + DSA_SUM_ROWS, nh), F32)],
        compiler_params=_cp("arbitrary", "arbitrary"),
        name="dsa",
    )(y0, y0, y0, y0, y0, kv_gain.reshape(1, -1), w_uk.astype(BF16), w_uv.astype(BF16))


def _nsa_cmp_kernel(x_ref, pos_ref, w1_ref, w2_ref, o_ref, *, nblk):
    half = NSA_CMP_LEN // 2
    a = jnp.zeros((nblk, HEAD_DIM), F32)
    b = jnp.zeros((nblk, HEAD_DIM), F32)
    for l in range(half):
        rows = x_ref[pl.ds(l, nblk, stride=NSA_CMP_STRIDE), :]
        a = a + _dot((rows + pos_ref[l:l + 1, :]).astype(BF16), w1_ref[l].astype(BF16))
        b = b + _dot((rows + pos_ref[l + half:l + half + 1, :]).astype(BF16),
                     w1_ref[l + half].astype(BF16))
    h = jax.nn.gelu(a + pltpu.roll(b, nblk - 1, axis=0))
    o_ref[...] = _dot(h.astype(BF16), w2_ref[...].astype(BF16))


def _nsa_compress(y0, pos, w1, w2):
    bsz, seq, _ = y0.shape
    nblk = seq // NSA_CMP_STRIDE
    kern = functools.partial(_nsa_cmp_kernel, nblk=nblk)
    return pl.pallas_call(
        kern,
        grid=(bsz, 2, B_KV_GROUPS),
        in_specs=[
            pl.BlockSpec((None, seq, HEAD_DIM), lambda b, t, g: (b, 0, L0_KV // HEAD_DIM + 2 * t + g)),
            pl.BlockSpec((None, NSA_CMP_LEN, HEAD_DIM), lambda b, t, g: (t, 0, 0)),
            pl.BlockSpec((None, NSA_CMP_LEN, HEAD_DIM, HEAD_DIM), lambda b, t, g: (t, 0, 0, 0)),
            pl.BlockSpec((None, HEAD_DIM, HEAD_DIM), lambda b, t, g: (t, 0, 0)),
        ],
        out_specs=pl.BlockSpec((None, None, None, nblk, HEAD_DIM), lambda b, t, g: (b, t, g, 0, 0)),
        out_shape=jax.ShapeDtypeStruct((bsz, 2, B_KV_GROUPS, nblk, HEAD_DIM), F32),
        compiler_params=_cp("arbitrary", "arbitrary", "arbitrary"),
        name="nsa_compress",
    )(y0, pos, w1, w2)


def _nsa_kernel(q_ref, gate_ref, kc_ref, vc_ref, ks_ref, vs_ref, kw_ref, vw_ref, ovl_ref, exp_ref,
                o_ref, ksb_ref, vst_ref, kwb_ref, vwt_ref, kcb_ref, vct_ref, m_ref, acc_ref, *, seq, nsel):
    i = pl.program_id(2)
    hpg = B_HEADS // B_KV_GROUPS
    ncmp = kc_ref.shape[0]
    wlen = NSA_WINDOW + Q_BLOCK
    d = HEAD_DIM

    @pl.when(i == 0)
    def _():
        ones = lambda n: jnp.ones((NSA_SUM_ROWS, n), BF16)
        ksb_ref[...] = ks_ref[...].astype(BF16)
        kwb_ref[...] = kw_ref[...].astype(BF16)
        kcb_ref[...] = kc_ref[...].astype(BF16)
        vst_ref[0:d, :] = vs_ref[...].T.astype(BF16)
        vst_ref[d:, :] = ones(seq)
        vwt_ref[0:d, :] = vw_ref[...].T.astype(BF16)
        vwt_ref[d:, :] = ones(seq)
        vct_ref[0:d, :] = vc_ref[...].T.astype(BF16)
        vct_ref[d:, :] = ones(ncmp)

    start = i * Q_BLOCK
    scale = HEAD_DIM ** -0.5
    qt = jnp.concatenate([(q_ref[:, h * d:(h + 1) * d] * scale).T for h in range(hpg)], axis=1).astype(BF16)
    qlane = start + lax.broadcasted_iota(jnp.int32, (1, Q_BLOCK), 1)
    qlane4 = jnp.concatenate([qlane] * hpg, axis=1)

    def softmax_pv(s, vt):
        m = jnp.max(s, axis=0, keepdims=True)
        ext = _dot(vt, jnp.exp(s - m))
        return ext[0:d, :] / ext[d:d + 1, :]

    s_c = _dot(kcb_ref[...], qt)
    cend = lax.broadcasted_iota(jnp.int32, (ncmp, 1), 0) * NSA_CMP_STRIDE + (NSA_CMP_LEN - 1)
    s_c = jnp.where(cend <= qlane4, s_c, NEG)
    p_c = jnp.exp(s_c - jnp.max(s_c, axis=0, keepdims=True))
    p_c = p_c / jnp.sum(p_c, axis=0, keepdims=True)
    p_c = jnp.where(qlane4 >= NSA_CMP_LEN - 1, p_c, 0.0)
    o_c = _dot(vct_ref[0:d, :], p_c.astype(BF16))

    psum = p_c[:, 0:Q_BLOCK]
    for h in range(1, hpg):
        psum = psum + p_c[:, h * Q_BLOCK:(h + 1) * Q_BLOCK]
    imp = jnp.dot(ovl_ref[...], psum, preferred_element_type=F32, precision=lax.Precision.HIGHEST)
    nslc = seq // NSA_SLC_LEN
    jj = lax.broadcasted_iota(jnp.int32, (LANES, 1), 0)
    cur = qlane // NSA_SLC_LEN
    forced = (jj == 0) | (jj == cur) | (jj == cur - 1)
    imp = jnp.where(forced, FORCE, imp)
    imp = jnp.where(jj <= cur, imp, NEG)
    imp = jnp.where(jj < nslc, imp, -3e38)
    rank = jnp.zeros((LANES, Q_BLOCK), F32)
    for t in range(nslc):
        row = imp[t:t + 1, :]
        later = jnp.where(jj > t, 1.0, 0.0)
        rank = rank + jnp.where(row > imp, 1.0, jnp.where(row == imp, later, 0.0))
    selb = jnp.where(rank < float(nsel), 1.0, 0.0).astype(BF16)

    kc = NSA_KEY_CHUNK
    nch = (start + Q_BLOCK + kc - 1) // kc
    kiota = lax.broadcasted_iota(jnp.int32, (kc, 1), 0)
    m_ref[...] = jnp.full(m_ref.shape, -jnp.inf, F32)
    acc_ref[...] = jnp.zeros_like(acc_ref)

    def attend(c, carry):
        ck = pl.ds(pl.multiple_of(c * kc, kc), kc)
        keysel = _dot(exp_ref[ck, :], selb)
        keep = (keysel > 0.5) & (c * kc + kiota <= qlane)
        bias = jnp.where(keep, 0.0, NEG).astype(BF16)
        s = _dot(ksb_ref[ck, :], qt).astype(BF16) + jnp.concatenate([bias] * hpg, axis=1)
        m_old = m_ref[...]
        m_new = jnp.maximum(m_old, jnp.max(s, axis=0, keepdims=True).astype(F32))
        p = jnp.exp(s - m_new.astype(BF16))
        acc_ref[...] = jnp.exp(m_old - m_new) * acc_ref[...] + _dot(vst_ref[:, ck], p)
        m_ref[...] = m_new
        return carry

    lax.fori_loop(0, nch, attend, 0)
    o_s = acc_ref[0:d, :] / acc_ref[d:d + 1, :]

    ws = pl.multiple_of(jnp.maximum(start - NSA_WINDOW, 0), Q_BLOCK)
    kposw = ws + lax.broadcasted_iota(jnp.int32, (wlen, 1), 0)
    vis_w = (kposw <= qlane4) & (kposw > qlane4 - NSA_WINDOW)
    s_w = _dot(kwb_ref[pl.ds(ws, wlen), :], qt).astype(BF16)
    o_w = softmax_pv(jnp.where(vis_w, s_w, jnp.asarray(NEG, BF16)), vwt_ref[:, pl.ds(ws, wlen)])

    gate = jax.nn.sigmoid(gate_ref[...]).T
    for h in range(hpg):
        cs = slice(h * Q_BLOCK, (h + 1) * Q_BLOCK)
        o = (gate[3 * h:3 * h + 1, :] * o_c[:, cs] + gate[3 * h + 1:3 * h + 2, :] * o_s[:, cs]
             + gate[3 * h + 2:3 * h + 3, :] * o_w[:, cs])
        o_ref[:, h * d:(h + 1) * d] = o.T


def _nsa(y0, kvc):
    bsz, seq, _ = y0.shape
    nb = seq // Q_BLOCK
    nslc = seq // NSA_SLC_LEN
    nsel = min(NSA_SLC_N, nslc)
    ncmp = seq // NSA_CMP_STRIDE
    hpg = B_HEADS // B_KV_GROUPS
    cs = np.arange(ncmp)[None, :] * NSA_CMP_STRIDE
    ss = np.arange(LANES)[:, None] * NSA_SLC_LEN
    ovl = ((cs < ss + NSA_SLC_LEN) & (cs + NSA_CMP_LEN > ss) & (np.arange(ncmp)[None, :] < ncmp - 1)
           & (np.arange(LANES)[:, None] < nslc)).astype(np.float32)
    expand = (np.arange(seq)[:, None] // NSA_SLC_LEN == np.arange(LANES)[None, :]).astype(np.float32)
    kvb = L0_KV // HEAD_DIM
    full = lambda off: pl.BlockSpec((None, seq, HEAD_DIM), lambda b, g, i, off=off: (b, 0, kvb + off + g))
    kern = functools.partial(_nsa_kernel, seq=seq, nsel=nsel)
    dx = HEAD_DIM + NSA_SUM_ROWS
    return pl.pallas_call(
        kern,
        grid=(bsz, B_KV_GROUPS, nb),
        in_specs=[
            pl.BlockSpec((None, Q_BLOCK, hpg * HEAD_DIM), lambda b, g, i: (b, i, L0_BQ // (hpg * HEAD_DIM) + g)),
            pl.BlockSpec((None, Q_BLOCK, LANES), lambda b, g, i: (b, i, L0_GATE // LANES + g)),
            pl.BlockSpec((None, None, None, ncmp, HEAD_DIM), lambda b, g, i: (b, 0, g, 0, 0)),
            pl.BlockSpec((None, None, None, ncmp, HEAD_DIM), lambda b, g, i: (b, 1, g, 0, 0)),
            full(4), full(6), full(8), full(10),
            pl.BlockSpec((LANES, ncmp), lambda b, g, i: (0, 0)),
            pl.BlockSpec((seq, LANES), lambda b, g, i: (0, 0)),
        ],
        out_specs=pl.BlockSpec((None, Q_BLOCK, hpg * HEAD_DIM), lambda b, g, i: (b, i, g)),
        out_shape=jax.ShapeDtypeStruct((bsz, seq, B_HEADS * HEAD_DIM), F32),
        scratch_shapes=[pltpu.VMEM((seq, HEAD_DIM), BF16), pltpu.VMEM((dx, seq), BF16),
                        pltpu.VMEM((seq, HEAD_DIM), BF16), pltpu.VMEM((dx, seq), BF16),
                        pltpu.VMEM((ncmp, HEAD_DIM), BF16), pltpu.VMEM((dx, ncmp), BF16),
                        pltpu.VMEM((1, hpg * Q_BLOCK), F32), pltpu.VMEM((dx, hpg * Q_BLOCK), F32)],
        compiler_params=_cp("arbitrary", "arbitrary", "arbitrary"),
        name="nsa",
    )(y0, y0, kvc, kvc, y0, y0, y0, y0, jnp.asarray(ovl), jnp.asarray(expand, dtype=BF16))


def _out_ln_kernel(a_ref, b_ref, wa_ref, wb_ref, x_ref, g_ref, beta_ref, o_ref):
    acc = _dot(a_ref[...].astype(BF16), wa_ref[...]) + _dot(b_ref[...].astype(BF16), wb_ref[...])
    o_ref[...] = _layer_norm(ALPHA * x_ref[...] + acc, g_ref[...], beta_ref[...])


def _out_ln(a, b, w_out, x, g, beta, tm=256):
    t, ka = a.shape
    kb = b.shape[1]
    d = x.shape[1]
    wa = w_out[:ka].astype(BF16)
    wb = w_out[ka:].astype(BF16)
    return pl.pallas_call(
        _out_ln_kernel,
        grid=(t // tm,),
        in_specs=[pl.BlockSpec((tm, ka), lambda i: (i, 0)), pl.BlockSpec((tm, kb), lambda i: (i, 0)),
                  pl.BlockSpec((ka, d), lambda i: (0, 0)), pl.BlockSpec((kb, d), lambda i: (0, 0)),
                  pl.BlockSpec((tm, d), lambda i: (i, 0)),
                  pl.BlockSpec((1, d), lambda i: (0, 0)), pl.BlockSpec((1, d), lambda i: (0, 0))],
        out_specs=pl.BlockSpec((tm, d), lambda i: (i, 0)),
        out_shape=jax.ShapeDtypeStruct((t, d), F32),
        compiler_params=_cp("arbitrary"),
        name="out_proj_ln",
    )(a, b, wa, wb, x, g.reshape(1, -1), beta.reshape(1, -1))


BIG = 1e30


def _pack_pair(v):
    b = lax.bitcast_convert_type(v.astype(BF16).astype(F32), jnp.int32)
    return b | lax.shift_right_logical(b, 16)


def _cmp_exchange(rows, i, l):
    a, b = rows[i], rows[l]
    rows[i], rows[l] = jnp.maximum(a, b), jnp.minimum(a, b)


def _bitonic_merge_desc(rows):
    n = len(rows)
    j = n // 2
    while j >= 1:
        for i in range(n):
            if i ^ j > i:
                _cmp_exchange(rows, i, i ^ j)
        j //= 2
    return rows


def _bitonic_sort_desc(rows):
    n = len(rows)
    k = 2
    while k <= n:
        j = k // 2
        while j >= 1:
            for i in range(n):
                l = i ^ j
                if l > i:
                    if (i & k) == 0:
                        _cmp_exchange(rows, i, l)
                    else:
                        _cmp_exchange(rows, l, i)
            j //= 2
        k *= 2
    return rows


def _top_sorted(rows):
    n = len(rows)
    rows = _bitonic_sort_desc(list(rows))
    shift = SUBLANES // 2
    while shift >= 1:
        other = [pltpu.roll(r, shift, axis=0) for r in rows]
        rows = _bitonic_merge_desc([jnp.maximum(rows[k], other[n - 1 - k]) for k in range(n)])
        shift //= 2
    return rows


def _sum_sublanes(x):
    shift = SUBLANES // 2
    while shift >= 1:
        x = x + pltpu.roll(x, shift, axis=0)
        shift //= 2
    return x


def _max_sublanes(x):
    shift = SUBLANES // 2
    while shift >= 1:
        x = jnp.maximum(x, pltpu.roll(x, shift, axis=0))
        shift //= 2
    return x


def _peer_stats_kernel(x_ref, wqt_ref, keys_ref, r1_ref, c1_ref, rk2_ref, e2_ref, qt_ref):
    qt_ref[...] = _dot_nt(wqt_ref[...], x_ref[...].astype(BF16))
    tn = x_ref.shape[0]
    half = PEER_DKEY // 2
    kh = PEER_TOPK_HALF
    nrow = PEER_NKEYS // SUBLANES
    sub = lax.broadcasted_iota(jnp.int32, (SUBLANES, 1), 0)
    tall = lambda v: jnp.concatenate([v] * nrow, axis=0)
    split = lambda s: [s[r * SUBLANES:(r + 1) * SUBLANES, :] for r in range(nrow)]

    def spread(vals):
        out = vals[0]
        for s in range(1, SUBLANES):
            out = jnp.where(sub == s, vals[s], out)
        return out

    for h in range(PEER_HEADS):
        r0 = h * PEER_DKEY
        s1 = _dot(keys_ref[0], qt_ref[r0:r0 + half, :].astype(BF16))
        s2 = _dot(keys_ref[1], qt_ref[r0 + half:r0 + PEER_DKEY, :].astype(BF16))
        v1 = _top_sorted(split(s1))
        v2 = _top_sorted(split(s2))
        b_lo, b_hi, a_hi = spread(v2[:SUBLANES]), spread(v2[SUBLANES:]), spread(v1[SUBLANES:])
        cand = [v1[0] + b_lo, v1[0] + b_hi, v1[1] + b_lo]
        for a in range(2, SUBLANES):
            cand.append(jnp.where(sub < (PEER_TOPK + 1) // (a + 1), v1[a] + b_lo, -BIG))
        cand.append(a_hi + v2[0])
        pad = jnp.full((SUBLANES, tn), -BIG, F32)
        cv = _top_sorted(cand + [pad] * (PEER_TOPK - len(cand)))
        top, kth = cv[0], cv[PEER_TOPK - 1]
        cnt = jnp.zeros((SUBLANES, tn), F32)
        z = jnp.zeros((SUBLANES, tn), F32)
        below = jnp.full((SUBLANES, tn), -BIG, F32)
        for c in cand:
            ge = c >= kth
            cnt = cnt + jnp.where(ge, 1.0, 0.0)
            z = z + jnp.where(ge, jnp.exp(c - top), 0.0)
            below = jnp.maximum(below, jnp.where(ge, -BIG, c))
        z = _sum_sublanes(z)
        nxt = jnp.where(_sum_sublanes(cnt) > float(PEER_TOPK), kth, _max_sublanes(below))
        thr = 0.5 * (kth + nxt)
        act1 = s1 >= tall(v1[kh - 1])
        act2 = s2 >= tall(v2[kh - 1])
        need = tall(thr) - s1
        r1 = jnp.zeros_like(s1)
        rk2 = jnp.zeros_like(s2)
        for k in range(kh):
            vk = tall(v2[k])
            r1 = r1 + jnp.where(vk >= need, 1.0, 0.0)
            rk2 = rk2 + jnp.where(vk > s2, 1.0, 0.0)
        r1_ref[h] = _pack_pair(jnp.where(act1, r1, 0.0))
        c1_ref[h] = _pack_pair(jnp.where(act1, jnp.exp(s1 - tall(v1[0])) / tall(z), 0.0))
        rk2_ref[h] = rk2.astype(BF16)
        e2_ref[h] = jnp.where(act2, jnp.exp(s2 - tall(v2[0])), 0.0).astype(BF16)


def _peer_stats(x, w_q, sub_keys, tn=512):
    t, d = x.shape
    wqt = w_q.T.astype(BF16)
    shp = lambda dt: jax.ShapeDtypeStruct((PEER_HEADS, PEER_NKEYS, t), dt)
    spec = pl.BlockSpec((PEER_HEADS, PEER_NKEYS, tn), lambda i: (0, 0, i))
    return pl.pallas_call(
        _peer_stats_kernel,
        grid=(t // tn,),
        in_specs=[pl.BlockSpec((tn, d), lambda i: (i, 0)),
                  pl.BlockSpec((PEER_HEADS * PEER_DKEY, d), lambda i: (0, 0)),
                  pl.BlockSpec((2, PEER_NKEYS, PEER_DKEY // 2), lambda i: (0, 0, 0))],
        out_specs=[spec] * 4,
        out_shape=[shp(jnp.int32), shp(jnp.int32), shp(BF16), shp(BF16)],
        scratch_shapes=[pltpu.VMEM((PEER_HEADS * PEER_DKEY, tn), F32)],
        compiler_params=_cp("arbitrary"),
        name="peer_stats",
    )(x, wqt, sub_keys.astype(BF16))


def _peer_kernel(x_ref, u_ref, vt_ref, r1_ref, c1_ref, rk2_ref, e2_ref, g_ref, beta_ref, o_ref,
                 xb_ref, acc_ref, w_ref, *, te):
    j = pl.program_id(1)
    tn = xb_ref.shape[0]
    pack = 2 * SUBLANES
    nsub = te // PEER_NKEYS

    @pl.when(j == 0)
    def _():
        xb_ref[...] = x_ref[...].astype(BF16)
        acc_ref[...] = jnp.zeros_like(acc_ref)

    def row_bf16(ref, h, idx):
        w = jnp.broadcast_to(ref[h, pl.ds(idx, 1), :], (SUBLANES, tn))
        return jnp.concatenate([pltpu.bitcast(w, BF16)] * (PEER_NKEYS // pack), axis=0)

    act = jax.nn.gelu(_dot_nt(u_ref[...], xb_ref[...]).astype(BF16))
    for ii in range(nsub):
        idx = j * nsub + ii
        gate = jnp.zeros((PEER_NKEYS, tn), BF16)
        for h in range(PEER_HEADS):
            sel = rk2_ref[h] < row_bf16(r1_ref, h, idx)
            gate = gate + jnp.where(sel, e2_ref[h], jnp.zeros((), BF16)) * row_bf16(c1_ref, h, idx)
        sl = slice(ii * PEER_NKEYS, (ii + 1) * PEER_NKEYS)
        w_ref[sl, :] = act[sl, :] * gate
    acc_ref[...] += _dot(vt_ref[...], w_ref[...])

    @pl.when(j == pl.num_programs(1) - 1)
    def _():
        y = ALPHA * x_ref[...] + acc_ref[...].T
        o_ref[...] = _layer_norm(y, g_ref[...], beta_ref[...])


def _peer_ln(x, stats, u_tab, v_tab, g, beta, tn=512, te=512):
    t, d = x.shape
    ne = u_tab.shape[0]
    ub = u_tab.astype(BF16)
    vtb = v_tab.T.astype(BF16)
    sspec = pl.BlockSpec((PEER_HEADS, PEER_NKEYS, tn), lambda i, j: (0, 0, i))
    kern = functools.partial(_peer_kernel, te=te)
    return pl.pallas_call(
        kern,
        grid=(t // tn, ne // te),
        in_specs=[pl.BlockSpec((tn, d), lambda i, j: (i, 0)),
                  pl.BlockSpec((te, d), lambda i, j: (j, 0)),
                  pl.BlockSpec((d, te), lambda i, j: (0, j)),
                  sspec, sspec, sspec, sspec,
                  pl.BlockSpec((1, d), lambda i, j: (0, 0)), pl.BlockSpec((1, d), lambda i, j: (0, 0))],
        out_specs=pl.BlockSpec((tn, d), lambda i, j: (i, 0)),
        out_shape=jax.ShapeDtypeStruct((t, d), F32),
        scratch_shapes=[pltpu.VMEM((tn, d), BF16), pltpu.VMEM((d, tn), F32), pltpu.VMEM((te, tn), BF16)],
        compiler_params=_cp("arbitrary", "arbitrary"),
        name="peer",
    )(x, ub, vtb, *stats, g.reshape(1, -1), beta.reshape(1, -1))


def _shortconv_kernel(cb_ref, cc_ref, ch_ref, ccp_ref, chp_ref, w_ref, o_ref, ext_ref, *, ts):
    i = pl.program_id(1)
    prev = ccp_ref[...] * chp_ref[...]
    ext_ref[0:SUBLANES, :] = jnp.where(i > 0, prev, 0.0)
    ext_ref[SUBLANES:, :] = cc_ref[...] * ch_ref[...]
    acc = jnp.zeros(o_ref.shape, F32)
    for k in range(CONV_W):
        off = SUBLANES - (CONV_W - 1) + k
        acc = acc + w_ref[k:k + 1, :] * ext_ref[off:off + ts, :]
    o_ref[...] = cb_ref[...] * acc


def _shortconv(y1, conv_w, ts=512):
    bsz, seq, _ = y1.shape
    blk = lambda c: pl.BlockSpec((None, ts, CONV_CH), lambda b, i, c=c: (b, i, c))
    halo = lambda c: pl.BlockSpec((None, SUBLANES, CONV_CH),
                                  lambda b, i, c=c: (b, jnp.maximum(i * (ts // SUBLANES) - 1, 0), c))
    kern = functools.partial(_shortconv_kernel, ts=ts)
    return pl.pallas_call(
        kern,
        grid=(bsz, seq // ts),
        in_specs=[blk(0), blk(1), blk(2), halo(1), halo(2),
                  pl.BlockSpec((CONV_W, CONV_CH), lambda b, i: (0, 0))],
        out_specs=pl.BlockSpec((None, ts, CONV_CH), lambda b, i: (b, i, 0)),
        out_shape=jax.ShapeDtypeStruct((bsz, seq, CONV_CH), F32),
        scratch_shapes=[pltpu.VMEM((ts + SUBLANES, CONV_CH), F32)],
        compiler_params=_cp("arbitrary", "arbitrary"),
        name="short_conv",
    )(y1, y1, y1, y1, y1, conv_w)


def _ssdconv_kernel(u_ref, up_ref, w_ref, b_ref, o_ref, ext_ref, *, ts):
    i = pl.program_id(1)
    ext_ref[0:SUBLANES, :] = jnp.where(i > 0, up_ref[...], 0.0)
    ext_ref[SUBLANES:, :] = u_ref[...]
    acc = jnp.zeros(o_ref.shape, F32) + b_ref[...]
    for k in range(SSD_CONV_W):
        off = SUBLANES - (SSD_CONV_W - 1) + k
        acc = acc + w_ref[k:k + 1, :] * ext_ref[off:off + ts, :]
    o_ref[...] = acc * jax.nn.sigmoid(acc)


def _ssdconv(y1, w, bias, ts=512, tc=1024):
    bsz, seq, _ = y1.shape
    nch = w.shape[1]
    c0 = L1_XS // tc
    kern = functools.partial(_ssdconv_kernel, ts=ts)
    return pl.pallas_call(
        kern,
        grid=(bsz, seq // ts, nch // tc),
        in_specs=[pl.BlockSpec((None, ts, tc), lambda b, i, c: (b, i, c0 + c)),
                  pl.BlockSpec((None, SUBLANES, tc),
                               lambda b, i, c: (b, jnp.maximum(i * (ts // SUBLANES) - 1, 0), c0 + c)),
                  pl.BlockSpec((SSD_CONV_W, tc), lambda b, i, c: (0, c)),
                  pl.BlockSpec((1, tc), lambda b, i, c: (0, c))],
        out_specs=pl.BlockSpec((None, ts, tc), lambda b, i, c: (b, i, c)),
        out_shape=jax.ShapeDtypeStruct((bsz, seq, nch), F32),
        scratch_shapes=[pltpu.VMEM((ts + SUBLANES, tc), F32)],
        compiler_params=_cp("arbitrary", "arbitrary", "arbitrary"),
        name="ssd_conv",
    )(y1, y1, w, bias.reshape(1, -1))


def _ssd_kernel(xs_ref, bs_ref, cs_ref, z_ref, dt_ref, dtb_ref, aneg_ref, dskip_ref, ng_ref, o_ref,
                state_ref):
    c = pl.program_id(2)
    lc = SSD_CHUNK
    npair = SSD_REP // 2

    @pl.when(c == 0)
    def _():
        state_ref[...] = jnp.zeros_like(state_ref)

    dt = jax.nn.softplus(dt_ref[...] + dtb_ref[...])
    da = dt * aneg_ref[...]
    ri = lax.broadcasted_iota(jnp.int32, (lc, lc), 0)
    ci = lax.broadcasted_iota(jnp.int32, (lc, lc), 1)
    causal = ci <= ri
    tril = jnp.where(causal, 1.0, 0.0)
    a_cs = jnp.dot(tril, da, preferred_element_type=F32, precision=lax.Precision.HIGHEST)
    a_cs_t = a_cs.T
    a_last = a_cs[lc - 1:lc, :]
    decay_s = jnp.exp(a_last - a_cs)
    decay_o = jnp.exp(a_cs)
    chunk_decay = jnp.exp(a_last)

    bc = bs_ref[...].astype(BF16)
    cc = cs_ref[...].astype(BF16)
    cb = _dot_nt(cc, bc)
    lane = lax.broadcasted_iota(jnp.int32, (lc, LANES), 1)
    lo_half = lane < SSD_HEAD_DIM
    row_lo = lax.broadcasted_iota(jnp.int32, (LANES, 1), 0) < SSD_HEAD_DIM

    hz = []
    ssq = jnp.zeros((lc, 1), F32)
    for pr in range(npair):
        r0, r1 = 2 * pr, 2 * pr + 1
        sl = slice(pr * LANES, (pr + 1) * LANES)
        x = xs_ref[:, sl]
        pick = lambda m: jnp.where(lo_half, m[:, r0:r0 + 1], m[:, r1:r1 + 1])
        xdt = x * pick(dt)
        y = jnp.zeros((lc, LANES), F32)
        for r, keep in ((r0, lo_half), (r1, ~lo_half)):
            seg = a_cs[:, r:r + 1] - a_cs_t[r:r + 1, :]
            m = cb * jnp.exp(jnp.where(causal, seg, NEG))
            y = y + _dot(m.astype(BF16), jnp.where(keep, xdt, 0.0).astype(BF16))
        prev = state_ref[sl, :]
        y = y + _dot_nt(cc, prev.astype(BF16)) * pick(decay_o)
        st = _dot_tn((xdt * pick(decay_s)).astype(BF16), bc)
        cd = jnp.where(row_lo, chunk_decay[:, r0:r0 + 1], chunk_decay[:, r1:r1 + 1])
        state_ref[sl, :] = prev * cd + st
        y = y + jnp.where(lo_half, dskip_ref[:, r0:r0 + 1], dskip_ref[:, r1:r1 + 1]) * x
        z = z_ref[:, sl]
        h = y * (z * jax.nn.sigmoid(z))
        hz.append(h)
        ssq = ssq + jnp.sum(h * h, axis=-1, keepdims=True)
    inv = lax.rsqrt(ssq / (SSD_REP * SSD_HEAD_DIM) + EPS)
    for pr in range(npair):
        sl = slice(pr * LANES, (pr + 1) * LANES)
        o_ref[:, sl] = hz[pr] * inv * ng_ref[:, sl]


def _ssd(y1, xbc, dt_bias_g, a_neg_g, d_skip_g, norm_g):
    bsz, seq, _ = y1.shape
    gw = SSD_REP * SSD_HEAD_DIM
    nchunk = seq // SSD_CHUNK
    vec = pl.BlockSpec((None, 1, LANES), lambda b, g, c: (g, 0, 0))
    return pl.pallas_call(
        _ssd_kernel,
        grid=(bsz, SSD_GROUPS, nchunk),
        in_specs=[
            pl.BlockSpec((None, SSD_CHUNK, gw), lambda b, g, c: (b, c, g)),
            pl.BlockSpec((None, SSD_CHUNK, SSD_STATE), lambda b, g, c: (b, c, SSD_INNER // SSD_STATE + g)),
            pl.BlockSpec((None, SSD_CHUNK, SSD_STATE),
                         lambda b, g, c: (b, c, SSD_INNER // SSD_STATE + SSD_GROUPS + g)),
            pl.BlockSpec((None, SSD_CHUNK, gw), lambda b, g, c: (b, c, L1_Z // gw + g)),
            pl.BlockSpec((None, SSD_CHUNK, LANES), lambda b, g, c: (b, c, L1_DT // LANES + g)),
            vec, vec, vec,
            pl.BlockSpec((1, gw), lambda b, g, c: (0, g)),
        ],
        out_specs=pl.BlockSpec((None, SSD_CHUNK, gw), lambda b, g, c: (b, c, g)),
        out_shape=jax.ShapeDtypeStruct((bsz, seq, SSD_INNER), F32),
        scratch_shapes=[pltpu.VMEM((gw, SSD_STATE), F32)],
        compiler_params=_cp("arbitrary", "arbitrary", "arbitrary"),
        name="ssd",
    )(xbc, xbc, xbc, y1, y1, dt_bias_g, a_neg_g, d_skip_g, norm_g.reshape(1, -1))


def _pad_cols(w, n):
    return jnp.pad(w, ((0, 0), (0, n - w.shape[1])))


def _pack_w_in0(w):
    o = np.cumsum((0, A_HEADS * HEAD_DIM, A_KV_RANK, IDX_HEADS * IDX_DIM, IDX_DIM, IDX_HEADS,
                   B_HEADS * HEAD_DIM) + (B_KV_GROUPS * HEAD_DIM,) * 6 + (B_HEADS * 3,))
    seg = lambda k: w[:, int(o[k]):int(o[k + 1])]
    gates = seg(12)
    ng = gates.shape[1] // B_KV_GROUPS
    parts = [seg(0), seg(2), seg(5)] + [seg(k) for k in range(6, 12)] + [
        seg(1), _pad_cols(seg(3), LANES), _pad_cols(seg(4), LANES)] + [
        _pad_cols(gates[:, g * ng:(g + 1) * ng], LANES) for g in range(B_KV_GROUPS)]
    return jnp.concatenate(parts, axis=1).astype(BF16)


def _pack_w_in1(w):
    dt = w[:, L1_DT:]
    parts = [w[:, :L1_DT]] + [_pad_cols(dt[:, g * SSD_REP:(g + 1) * SSD_REP], LANES) for g in range(SSD_GROUPS)]
    return jnp.concatenate(parts, axis=1).astype(BF16)


def _per_group(v):
    return jnp.pad(v.reshape(SSD_GROUPS, 1, SSD_REP), ((0, 0), (0, 0), (0, LANES - SSD_REP)))


def _peer_block(x, wq, keys, u, v, g, b):
    stats = _peer_stats(x, wq, keys)
    return _peer_ln(x, stats, u, v, g, b)


def _even_layer(x, bsz, seq, w_in, kv_gain, w_uk, w_uv, cmpk_pos, cmpk_w1, cmpk_w2, cmpv_pos, cmpv_w1, cmpv_w2,
                w_out, ln1_g, ln1_b, peer_wq, peer_keys, peer_u, peer_v, ln2_g, ln2_b):
    t = bsz * seq
    y0 = _matmul(x, _pack_w_in0(w_in)).reshape(bsz, seq, NP0)
    o_a = _dsa(y0, kv_gain, w_uk, w_uv)
    kvc = _nsa_compress(y0, jnp.stack([cmpk_pos, cmpv_pos]), jnp.stack([cmpk_w1, cmpv_w1]),
                        jnp.stack([cmpk_w2, cmpv_w2]))
    o_b = _nsa(y0, kvc)
    x = _out_ln(o_a.reshape(t, -1), o_b.reshape(t, -1), w_out, x, ln1_g, ln1_b)
    return _peer_block(x, peer_wq, peer_keys, peer_u, peer_v, ln2_g, ln2_b)


def _odd_layer(x, bsz, seq, w_in, conv_w, ssd_conv_w, ssd_conv_b, dt_bias, a_log, d_skip, norm_g,
               w_out, ln1_g, ln1_b, peer_wq, peer_keys, peer_u, peer_v, ln2_g, ln2_b):
    t = bsz * seq
    y1 = _matmul(x, _pack_w_in1(w_in)).reshape(bsz, seq, NP1)
    y_c = _shortconv(y1, conv_w)
    xbc = _ssdconv(y1, ssd_conv_w, ssd_conv_b)
    y_d = _ssd(y1, xbc, _per_group(dt_bias), _per_group(-jnp.exp(a_log)), _per_group(d_skip), norm_g)
    x = _out_ln(y_c.reshape(t, -1), y_d.reshape(t, -1), w_out, x, ln1_g, ln1_b)
    return _peer_block(x, peer_wq, peer_keys, peer_u, peer_v, ln2_g, ln2_b)


def kernel(x, l0_w_in, l0_kv_gain, l0_w_uk, l0_w_uv, l0_cmpk_pos, l0_cmpk_w1, l0_cmpk_w2, l0_cmpv_pos, l0_cmpv_w1, l0_cmpv_w2, l0_w_out, l0_ln1_g, l0_ln1_b, l0_peer_wq, l0_peer_keys, l0_peer_u, l0_peer_v, l0_ln2_g, l0_ln2_b, l1_w_in, l1_conv_w, l1_ssd_conv_w, l1_ssd_conv_b, l1_dt_bias, l1_a_log, l1_d_skip, l1_norm_g, l1_w_out, l1_ln1_g, l1_ln1_b, l1_peer_wq, l1_peer_keys, l1_peer_u, l1_peer_v, l1_ln2_g, l1_ln2_b):
    bsz, seq, d = x.shape
    h = x.reshape(bsz * seq, d)
    h = _even_layer(h, bsz, seq, l0_w_in, l0_kv_gain, l0_w_uk, l0_w_uv, l0_cmpk_pos, l0_cmpk_w1, l0_cmpk_w2,
                    l0_cmpv_pos, l0_cmpv_w1, l0_cmpv_w2, l0_w_out, l0_ln1_g, l0_ln1_b,
                    l0_peer_wq, l0_peer_keys, l0_peer_u, l0_peer_v, l0_ln2_g, l0_ln2_b)
    h = _odd_layer(h, bsz, seq, l1_w_in, l1_conv_w, l1_ssd_conv_w, l1_ssd_conv_b, l1_dt_bias, l1_a_log,
                   l1_d_skip, l1_norm_g, l1_w_out, l1_ln1_g, l1_ln1_b,
                   l1_peer_wq, l1_peer_keys, l1_peer_u, l1_peer_v, l1_ln2_g, l1_ln2_b)
    return h.reshape(bsz, seq, d)
```

```python
import functools

import numpy as np
import jax
import jax.numpy as jnp
from jax import lax
from jax.experimental import pallas as pl
from jax.experimental.pallas import tpu as pltpu

F32 = jnp.float32
BF16 = jnp.bfloat16

D_MODEL = 2048
HEAD_DIM = 128
Q_BLOCK = 128
NEG = -1e30
FORCE = 1e30
EPS = 1e-5

A_HEADS = 8
A_KV_RANK = 256
IDX_HEADS = 16
IDX_DIM = 64
DSA_TOPK_MAX = 256
DSA_KEY_CHUNK = 512
DSA_SUM_ROWS = 16

B_HEADS = 8
B_KV_GROUPS = 2
NSA_CMP_LEN = 32
NSA_CMP_STRIDE = 16
NSA_SLC_LEN = 64
NSA_SLC_N = 16
NSA_WINDOW = 512
NSA_KEY_CHUNK = 512
NSA_SUM_ROWS = 16

CONV_CH = 1024
CONV_W = 3

SSD_HEAD_DIM = 64
SSD_HEADS = 48
SSD_INNER = SSD_HEADS * SSD_HEAD_DIM
SSD_GROUPS = 8
SSD_STATE = 128
SSD_CONV_W = 4
SSD_CHUNK = 128
SSD_REP = SSD_HEADS // SSD_GROUPS

PEER_HEADS = 8
PEER_NKEYS = 128
PEER_N = PEER_NKEYS * PEER_NKEYS
PEER_DKEY = 256
PEER_TOPK_HALF = 16
PEER_TOPK = 16

DEPTH = 2
ALPHA = (2 * DEPTH) ** 0.25

LANES = 128
SUBLANES = 8
VMEM_LIMIT = 56 * 1024 * 1024

L0_Q, L0_IQ, L0_BQ, L0_KV, L0_CKV, L0_IK, L0_IW, L0_GATE = 0, 1024, 2048, 3072, 4608, 4864, 4992, 5120
NP0 = 5376
L1_Z, L1_XS, L1_BS, L1_CS, L1_DT = 3072, 6144, 9216, 10240, 11264
NP1 = L1_DT + SSD_GROUPS * LANES


def _cp(*sem):
    return pltpu.CompilerParams(dimension_semantics=sem, vmem_limit_bytes=VMEM_LIMIT)


def _dot(a, b):
    return jnp.dot(a, b, preferred_element_type=F32)


def _dot_nt(a, b):
    return lax.dot_general(a, b, (((1,), (1,)), ((), ())), preferred_element_type=F32)


def _dot_tn(a, b):
    return lax.dot_general(a, b, (((0,), (0,)), ((), ())), preferred_element_type=F32)


def _layer_norm(v, g, b):
    mu = jnp.mean(v, axis=-1, keepdims=True)
    d = v - mu
    var = jnp.mean(d * d, axis=-1, keepdims=True)
    return d * lax.rsqrt(var + EPS) * g + b


def _mm_kernel(x_ref, w_ref, o_ref, xb_ref):
    @pl.when(pl.program_id(1) == 0)
    def _():
        xb_ref[...] = x_ref[...].astype(BF16)

    o_ref[...] = _dot(xb_ref[...], w_ref[...])


def _matmul(x, w, tm=1024, tn=768):
    t, k = x.shape
    n = w.shape[1]
    return pl.pallas_call(
        _mm_kernel,
        grid=(t // tm, n // tn),
        in_specs=[pl.BlockSpec((tm, k), lambda i, j: (i, 0)),
                  pl.BlockSpec((k, tn), lambda i, j: (0, j))],
        out_specs=pl.BlockSpec((tm, tn), lambda i, j: (i, j)),
        out_shape=jax.ShapeDtypeStruct((t, n), F32),
        scratch_shapes=[pltpu.VMEM((tm, k), BF16)],
        compiler_params=_cp("arbitrary", "arbitrary"),
        name="in_proj",
    )(x, w)


def _dsa_kernel(q_ref, iq_ref, iw_ref, ckv_ref, ik_ref, gain_ref, wuk_ref, wuv_ref, o_ref,
                ckvn_ref, ckvt_ref, iklo_ref, ikhi_ref, key_ref, bias_ref, qlt_ref, m_ref, acc_ref,
                *, seq, topk):
    i = pl.program_id(1)

    @pl.when(i == 0)
    def _():
        c = ckv_ref[...]
        ms = jnp.mean(c * c, axis=-1, keepdims=True)
        cn = c * lax.rsqrt(ms + EPS) * gain_ref[...]
        ckvn_ref[...] = cn.astype(BF16)
        ckvt_ref[0:A_KV_RANK, :] = cn.T.astype(BF16)
        ckvt_ref[A_KV_RANK:, :] = jnp.ones((DSA_SUM_ROWS, seq), BF16)
        ik = ik_ref[...]
        iklo_ref[...] = ik.astype(BF16)
        ikhi_ref[...] = pltpu.roll(ik, IDX_DIM, axis=1).astype(BF16)

    kc = DSA_KEY_CHUNK
    nch = (i * Q_BLOCK + Q_BLOCK + kc - 1) // kc
    qpos = i * Q_BLOCK + lax.broadcasted_iota(jnp.int32, (kc, Q_BLOCK), 1)
    kiota = lax.broadcasted_iota(jnp.int32, (kc, Q_BLOCK), 0)

    def chunk(c):
        return pl.ds(pl.multiple_of(c * kc, kc), kc)

    iwt = iw_ref[...].T * ((IDX_DIM * IDX_HEADS) ** -0.5)
    npair = IDX_HEADS // 2
    iq2 = [jnp.concatenate([iq_ref[:, p * LANES:(p + 1) * LANES], iq_ref[:, (p + 1) * LANES:(p + 2) * LANES]],
                           axis=0).astype(BF16) for p in range(0, npair, 2)]

    def index_chunk(c, carry):
        lo = iklo_ref[chunk(c), :]
        hi = ikhi_ref[chunk(c), :]
        score = jnp.zeros((kc, Q_BLOCK), F32)
        for t, iqt in enumerate(iq2):
            d_lo = jnp.maximum(_dot_nt(lo, iqt), 0.0)
            d_hi = jnp.maximum(_dot_nt(hi, iqt), 0.0)
            for u in range(2):
                h0 = 4 * t + 2 * u
                score = score + d_lo[:, u * Q_BLOCK:(u + 1) * Q_BLOCK] * iwt[h0:h0 + 1, :]
                score = score + d_hi[:, u * Q_BLOCK:(u + 1) * Q_BLOCK] * iwt[h0 + 1:h0 + 2, :]
        score = jnp.where(c * kc + kiota <= qpos, score, NEG)
        bits = lax.bitcast_convert_type(score, jnp.int32)
        key_ref[chunk(c), :] = jnp.where(bits < 0, bits ^ jnp.int32(0x7FFFFFFF), bits)
        return carry

    lax.fori_loop(0, nch, index_chunk, 0)

    def count_ge(cand):
        def body(c, acc):
            ge = jnp.where(key_ref[chunk(c), :] >= cand, 1.0, 0.0)
            while ge.shape[0] > SUBLANES:
                half = ge.shape[0] // 2
                ge = ge[:half] + ge[half:]
            return acc + ge
        acc = lax.fori_loop(0, nch, body, jnp.zeros((SUBLANES, Q_BLOCK), F32))
        return jnp.sum(acc, axis=0, keepdims=True)

    kf = float(topk)
    zero = jnp.zeros((1, Q_BLOCK), jnp.int32)
    prefix = jnp.where(count_ge(zero) >= kf, zero, jnp.int32(-2 ** 31))

    def bit_step(t, prefix):
        cand = prefix + lax.shift_left(jnp.int32(1), 30 - t)
        return jnp.where(count_ge(cand) >= kf, cand, prefix)

    prefix = lax.fori_loop(0, 31, bit_step, prefix)

    def bias_chunk(c, carry):
        keep = (key_ref[chunk(c), :] >= prefix) & (c * kc + kiota <= qpos)
        bias_ref[chunk(c), :] = jnp.where(keep, 0.0, NEG).astype(BF16)
        return carry

    lax.fori_loop(0, nch, bias_chunk, 0)

    scale = HEAD_DIM ** -0.5
    for h in range(A_HEADS):
        qh = q_ref[:, h * HEAD_DIM:(h + 1) * HEAD_DIM].astype(BF16)
        qlt_ref[:, h * Q_BLOCK:(h + 1) * Q_BLOCK] = (_dot_nt(wuk_ref[h], qh) * scale).astype(BF16)
    m_ref[...] = jnp.full(m_ref.shape, -jnp.inf, F32)
    acc_ref[...] = jnp.zeros_like(acc_ref)

    def attend(c, carry):
        s = _dot(ckvn_ref[chunk(c), :], qlt_ref[...]).astype(BF16)
        s = s + jnp.concatenate([bias_ref[chunk(c), :]] * A_HEADS, axis=1)
        m_old = m_ref[...]
        m_new = jnp.maximum(m_old, jnp.max(s, axis=0, keepdims=True).astype(F32))
        p = jnp.exp(s - m_new.astype(BF16))
        alpha = jnp.exp(m_old - m_new)
        acc_ref[...] = alpha * acc_ref[...] + _dot(ckvt_ref[:, chunk(c)], p)
        m_ref[...] = m_new
        return carry

    lax.fori_loop(0, nch, attend, 0)
    olt = (acc_ref[0:A_KV_RANK, :] / acc_ref[A_KV_RANK:A_KV_RANK + 1, :]).astype(BF16)
    for h in range(A_HEADS):
        o_ref[:, h * HEAD_DIM:(h + 1) * HEAD_DIM] = _dot_tn(olt[:, h * Q_BLOCK:(h + 1) * Q_BLOCK], wuv_ref[h])


def _dsa(y0, kv_gain, w_uk, w_uv):
    bsz, seq, _ = y0.shape
    topk = min(DSA_TOPK_MAX, seq // 4)
    nb = seq // Q_BLOCK
    nh = A_HEADS * Q_BLOCK
    kern = functools.partial(_dsa_kernel, seq=seq, topk=topk)
    return pl.pallas_call(
        kern,
        grid=(bsz, nb),
        in_specs=[
            pl.BlockSpec((None, Q_BLOCK, 1024), lambda b, i: (b, i, L0_Q // 1024)),
            pl.BlockSpec((None, Q_BLOCK, 1024), lambda b, i: (b, i, L0_IQ // 1024)),
            pl.BlockSpec((None, Q_BLOCK, LANES), lambda b, i: (b, i, L0_IW // LANES)),
            pl.BlockSpec((None, seq, A_KV_RANK), lambda b, i: (b, 0, L0_CKV // A_KV_RANK)),
            pl.BlockSpec((None, seq, LANES), lambda b, i: (b, 0, L0_IK // LANES)),
            pl.BlockSpec((1, A_KV_RANK), lambda b, i: (0, 0)),
            pl.BlockSpec((A_HEADS, A_KV_RANK, HEAD_DIM), lambda b, i: (0, 0, 0)),
            pl.BlockSpec((A_HEADS, A_KV_RANK, HEAD_DIM), lambda b, i: (0, 0, 0)),
        ],
        out_specs=pl.BlockSpec((None, Q_BLOCK, A_HEADS * HEAD_DIM), lambda b, i: (b, i, 0)),
        out_shape=jax.ShapeDtypeStruct((bsz, seq, A_HEADS * HEAD_DIM), F32),
        scratch_shapes=[pltpu.VMEM((seq, A_KV_RANK), BF16), pltpu.VMEM((A_KV_RANK + DSA_SUM_ROWS, seq), BF16),
                        pltpu.VMEM((seq, LANES), BF16), pltpu.VMEM((seq, LANES), BF16),
                        pltpu.VMEM((seq, Q_BLOCK), jnp.int32), pltpu.VMEM((seq, Q_BLOCK), BF16),
                        pltpu.VMEM((A_KV_RANK, nh), BF16),
                        pltpu.VMEM((1, nh), F32),
                        pltpu.VMEM((A_KV_RANK + DSA_SUM_ROWS, nh), F32)],
        compiler_params=_cp("arbitrary", "arbitrary"),
        name="dsa",
    )(y0, y0, y0, y0, y0, kv_gain.reshape(1, -1), w_uk.astype(BF16), w_uv.astype(BF16))


def _nsa_cmp_kernel(x_ref, pos_ref, w1_ref, w2_ref, o_ref, *, nblk):
    half = NSA_CMP_LEN // 2
    a = jnp.zeros((nblk, HEAD_DIM), F32)
    b = jnp.zeros((nblk, HEAD_DIM), F32)
    for l in range(half):
        rows = x_ref[pl.ds(l, nblk, stride=NSA_CMP_STRIDE), :]
        a = a + _dot((rows + pos_ref[l:l + 1, :]).astype(BF16), w1_ref[l].astype(BF16))
        b = b + _dot((rows + pos_ref[l + half:l + half + 1, :]).astype(BF16),
                     w1_ref[l + half].astype(BF16))
    h = jax.nn.gelu(a + pltpu.roll(b, nblk - 1, axis=0))
    o_ref[...] = _dot(h.astype(BF16), w2_ref[...].astype(BF16))


def _nsa_compress(y0, pos, w1, w2):
    bsz, seq, _ = y0.shape
    nblk = seq // NSA_CMP_STRIDE
    kern = functools.partial(_nsa_cmp_kernel, nblk=nblk)
    return pl.pallas_call(
        kern,
        grid=(bsz, 2, B_KV_GROUPS),
        in_specs=[
            pl.BlockSpec((None, seq, HEAD_DIM), lambda b, t, g: (b, 0, L0_KV // HEAD_DIM + 2 * t + g)),
            pl.BlockSpec((None, NSA_CMP_LEN, HEAD_DIM), lambda b, t, g: (t, 0, 0)),
            pl.BlockSpec((None, NSA_CMP_LEN, HEAD_DIM, HEAD_DIM), lambda b, t, g: (t, 0, 0, 0)),
            pl.BlockSpec((None, HEAD_DIM, HEAD_DIM), lambda b, t, g: (t, 0, 0)),
        ],
        out_specs=pl.BlockSpec((None, None, None, nblk, HEAD_DIM), lambda b, t, g: (b, t, g, 0, 0)),
        out_shape=jax.ShapeDtypeStruct((bsz, 2, B_KV_GROUPS, nblk, HEAD_DIM), F32),
        compiler_params=_cp("arbitrary", "arbitrary", "arbitrary"),
        name="nsa_compress",
    )(y0, pos, w1, w2)


def _nsa_kernel(q_ref, gate_ref, kc_ref, vc_ref, ks_ref, vs_ref, kw_ref, vw_ref, ovl_ref, exp_ref,
                o_ref, ksb_ref, vst_ref, kwb_ref, vwt_ref, kcb_ref, vct_ref, m_ref, acc_ref, *, seq, nsel):
    i = pl.program_id(2)
    hpg = B_HEADS // B_KV_GROUPS
    ncmp = kc_ref.shape[0]
    wlen = NSA_WINDOW + Q_BLOCK
    d = HEAD_DIM

    @pl.when(i == 0)
    def _():
        ones = lambda n: jnp.ones((NSA_SUM_ROWS, n), BF16)
        ksb_ref[...] = ks_ref[...].astype(BF16)
        kwb_ref[...] = kw_ref[...].astype(BF16)
        kcb_ref[...] = kc_ref[...].astype(BF16)
        vst_ref[0:d, :] = vs_ref[...].T.astype(BF16)
        vst_ref[d:, :] = ones(seq)
        vwt_ref[0:d, :] = vw_ref[...].T.astype(BF16)
        vwt_ref[d:, :] = ones(seq)
        vct_ref[0:d, :] = vc_ref[...].T.astype(BF16)
        vct_ref[d:, :] = ones(ncmp)

    start = i * Q_BLOCK
    scale = HEAD_DIM ** -0.5
    qt = jnp.concatenate([(q_ref[:, h * d:(h + 1) * d] * scale).T for h in range(hpg)], axis=1).astype(BF16)
    qlane = start + lax.broadcasted_iota(jnp.int32, (1, Q_BLOCK), 1)
    qlane4 = jnp.concatenate([qlane] * hpg, axis=1)

    def softmax_pv(s, vt):
        m = jnp.max(s, axis=0, keepdims=True)
        ext = _dot(vt, jnp.exp(s - m))
        return ext[0:d, :] / ext[d:d + 1, :]

    s_c = _dot(kcb_ref[...], qt)
    cend = lax.broadcasted_iota(jnp.int32, (ncmp, 1), 0) * NSA_CMP_STRIDE + (NSA_CMP_LEN - 1)
    s_c = jnp.where(cend <= qlane4, s_c, NEG)
    p_c = jnp.exp(s_c - jnp.max(s_c, axis=0, keepdims=True))
    p_c = p_c / jnp.sum(p_c, axis=0, keepdims=True)
    p_c = jnp.where(qlane4 >= NSA_CMP_LEN - 1, p_c, 0.0)
    o_c = _dot(vct_ref[0:d, :], p_c.astype(BF16))

    psum = p_c[:, 0:Q_BLOCK]
    for h in range(1, hpg):
        psum = psum + p_c[:, h * Q_BLOCK:(h + 1) * Q_BLOCK]
    imp = jnp.dot(ovl_ref[...], psum, preferred_element_type=F32, precision=lax.Precision.HIGHEST)
    nslc = seq // NSA_SLC_LEN
    jj = lax.broadcasted_iota(jnp.int32, (LANES, 1), 0)
    cur = qlane // NSA_SLC_LEN
    forced = (jj == 0) | (jj == cur) | (jj == cur - 1)
    imp = jnp.where(forced, FORCE, imp)
    imp = jnp.where(jj <= cur, imp, NEG)
    imp = jnp.where(jj < nslc, imp, -3e38)
    rank = jnp.zeros((LANES, Q_BLOCK), F32)
    for t in range(nslc):
        row = imp[t:t + 1, :]
        later = jnp.where(jj > t, 1.0, 0.0)
        rank = rank + jnp.where(row > imp, 1.0, jnp.where(row == imp, later, 0.0))
    selb = jnp.where(rank < float(nsel), 1.0, 0.0).astype(BF16)

    kc = NSA_KEY_CHUNK
    nch = (start + Q_BLOCK + kc - 1) // kc
    kiota = lax.broadcasted_iota(jnp.int32, (kc, 1), 0)
    m_ref[...] = jnp.full(m_ref.shape, -jnp.inf, F32)
    acc_ref[...] = jnp.zeros_like(acc_ref)

    def attend(c, carry):
        ck = pl.ds(pl.multiple_of(c * kc, kc), kc)
        keysel = _dot(exp_ref[ck, :], selb)
        keep = (keysel > 0.5) & (c * kc + kiota <= qlane)
        bias = jnp.where(keep, 0.0, NEG).astype(BF16)
        s = _dot(ksb_ref[ck, :], qt).astype(BF16) + jnp.concatenate([bias] * hpg, axis=1)
        m_old = m_ref[...]
        m_new = jnp.maximum(m_old, jnp.max(s, axis=0, keepdims=True).astype(F32))
        p = jnp.exp(s - m_new.astype(BF16))
        acc_ref[...] = jnp.exp(m_old - m_new) * acc_ref[...] + _dot(vst_ref[:, ck], p)
        m_ref[...] = m_new
        return carry

    lax.fori_loop(0, nch, attend, 0)
    o_s = acc_ref[0:d, :] / acc_ref[d:d + 1, :]

    ws = pl.multiple_of(jnp.maximum(start - NSA_WINDOW, 0), Q_BLOCK)
    kposw = ws + lax.broadcasted_iota(jnp.int32, (wlen, 1), 0)
    vis_w = (kposw <= qlane4) & (kposw > qlane4 - NSA_WINDOW)
    s_w = _dot(kwb_ref[pl.ds(ws, wlen), :], qt).astype(BF16)
    o_w = softmax_pv(jnp.where(vis_w, s_w, jnp.asarray(NEG, BF16)), vwt_ref[:, pl.ds(ws, wlen)])

    gate = jax.nn.sigmoid(gate_ref[...]).T
    for h in range(hpg):
        cs = slice(h * Q_BLOCK, (h + 1) * Q_BLOCK)
        o = (gate[3 * h:3 * h + 1, :] * o_c[:, cs] + gate[3 * h + 1:3 * h + 2, :] * o_s[:, cs]
             + gate[3 * h + 2:3 * h + 3, :] * o_w[:, cs])
        o_ref[:, h * d:(h + 1) * d] = o.T


def _nsa(y0, kvc):
    bsz, seq, _ = y0.shape
    nb = seq // Q_BLOCK
    nslc = seq // NSA_SLC_LEN
    nsel = min(NSA_SLC_N, nslc)
    ncmp = seq // NSA_CMP_STRIDE
    hpg = B_HEADS // B_KV_GROUPS
    cs = np.arange(ncmp)[None, :] * NSA_CMP_STRIDE
    ss = np.arange(LANES)[:, None] * NSA_SLC_LEN
    ovl = ((cs < ss + NSA_SLC_LEN) & (cs + NSA_CMP_LEN > ss) & (np.arange(ncmp)[None, :] < ncmp - 1)
           & (np.arange(LANES)[:, None] < nslc)).astype(np.float32)
    expand = (np.arange(seq)[:, None] // NSA_SLC_LEN == np.arange(LANES)[None, :]).astype(np.float32)
    kvb = L0_KV // HEAD_DIM
    full = lambda off: pl.BlockSpec((None, seq, HEAD_DIM), lambda b, g, i, off=off: (b, 0, kvb + off + g))
    kern = functools.partial(_nsa_kernel, seq=seq, nsel=nsel)
    dx = HEAD_DIM + NSA_SUM_ROWS
    return pl.pallas_call(
        kern,
        grid=(bsz, B_KV_GROUPS, nb),
        in_specs=[
            pl.BlockSpec((None, Q_BLOCK, hpg * HEAD_DIM), lambda b, g, i: (b, i, L0_BQ // (hpg * HEAD_DIM) + g)),
            pl.BlockSpec((None, Q_BLOCK, LANES), lambda b, g, i: (b, i, L0_GATE // LANES + g)),
            pl.BlockSpec((None, None, None, ncmp, HEAD_DIM), lambda b, g, i: (b, 0, g, 0, 0)),
            pl.BlockSpec((None, None, None, ncmp, HEAD_DIM), lambda b, g, i: (b, 1, g, 0, 0)),
            full(4), full(6), full(8), full(10),
            pl.BlockSpec((LANES, ncmp), lambda b, g, i: (0, 0)),
            pl.BlockSpec((seq, LANES), lambda b, g, i: (0, 0)),
        ],
        out_specs=pl.BlockSpec((None, Q_BLOCK, hpg * HEAD_DIM), lambda b, g, i: (b, i, g)),
        out_shape=jax.ShapeDtypeStruct((bsz, seq, B_HEADS * HEAD_DIM), F32),
        scratch_shapes=[pltpu.VMEM((seq, HEAD_DIM), BF16), pltpu.VMEM((dx, seq), BF16),
                        pltpu.VMEM((seq, HEAD_DIM), BF16), pltpu.VMEM((dx, seq), BF16),
                        pltpu.VMEM((ncmp, HEAD_DIM), BF16), pltpu.VMEM((dx, ncmp), BF16),
                        pltpu.VMEM((1, hpg * Q_BLOCK), F32), pltpu.VMEM((dx, hpg * Q_BLOCK), F32)],
        compiler_params=_cp("arbitrary", "arbitrary", "arbitrary"),
        name="nsa",
    )(y0, y0, kvc, kvc, y0, y0, y0, y0, jnp.asarray(ovl), jnp.asarray(expand, dtype=BF16))


def _out_ln_kernel(a_ref, b_ref, wa_ref, wb_ref, x_ref, g_ref, beta_ref, o_ref):
    acc = _dot(a_ref[...].astype(BF16), wa_ref[...]) + _dot(b_ref[...].astype(BF16), wb_ref[...])
    o_ref[...] = _layer_norm(ALPHA * x_ref[...] + acc, g_ref[...], beta_ref[...])


def _out_ln(a, b, w_out, x, g, beta, tm=256):
    t, ka = a.shape
    kb = b.shape[1]
    d = x.shape[1]
    wa = w_out[:ka].astype(BF16)
    wb = w_out[ka:].astype(BF16)
    return pl.pallas_call(
        _out_ln_kernel,
        grid=(t // tm,),
        in_specs=[pl.BlockSpec((tm, ka), lambda i: (i, 0)), pl.BlockSpec((tm, kb), lambda i: (i, 0)),
                  pl.BlockSpec((ka, d), lambda i: (0, 0)), pl.BlockSpec((kb, d), lambda i: (0, 0)),
                  pl.BlockSpec((tm, d), lambda i: (i, 0)),
                  pl.BlockSpec((1, d), lambda i: (0, 0)), pl.BlockSpec((1, d), lambda i: (0, 0))],
        out_specs=pl.BlockSpec((tm, d), lambda i: (i, 0)),
        out_shape=jax.ShapeDtypeStruct((t, d), F32),
        compiler_params=_cp("arbitrary"),
        name="out_proj_ln",
    )(a, b, wa, wb, x, g.reshape(1, -1), beta.reshape(1, -1))


BIG = 1e30


def _pack_pair(v):
    b = lax.bitcast_convert_type(v.astype(BF16).astype(F32), jnp.int32)
    return b | lax.shift_right_logical(b, 16)


def _cmp_exchange(rows, i, l):
    a, b = rows[i], rows[l]
    rows[i], rows[l] = jnp.maximum(a, b), jnp.minimum(a, b)


def _bitonic_merge_desc(rows):
    n = len(rows)
    j = n // 2
    while j >= 1:
        for i in range(n):
            if i ^ j > i:
                _cmp_exchange(rows, i, i ^ j)
        j //= 2
    return rows


def _bitonic_sort_desc(rows):
    n = len(rows)
    k = 2
    while k <= n:
        j = k // 2
        while j >= 1:
            for i in range(n):
                l = i ^ j
                if l > i:
                    if (i & k) == 0:
                        _cmp_exchange(rows, i, l)
                    else:
                        _cmp_exchange(rows, l, i)
            j //= 2
        k *= 2
    return rows


def _top_sorted(rows):
    n = len(rows)
    rows = _bitonic_sort_desc(list(rows))
    shift = SUBLANES // 2
    while shift >= 1:
        other = [pltpu.roll(r, shift, axis=0) for r in rows]
        rows = _bitonic_merge_desc([jnp.maximum(rows[k], other[n - 1 - k]) for k in range(n)])
        shift //= 2
    return rows


def _sum_sublanes(x):
    shift = SUBLANES // 2
    while shift >= 1:
        x = x + pltpu.roll(x, shift, axis=0)
        shift //= 2
    return x


def _max_sublanes(x):
    shift = SUBLANES // 2
    while shift >= 1:
        x = jnp.maximum(x, pltpu.roll(x, shift, axis=0))
        shift //= 2
    return x


def _count_sorted(vals, x, pred):
    n = len(vals)
    step = n // 2
    base = jnp.zeros_like(x)
    hist = []
    while step >= 1:
        options = [vals[b + step - 1] for b in range(0, n, 2 * step)]

        def choose(level, lo_idx):
            if level == len(hist):
                return options[lo_idx]
            span = len(options) >> (level + 1)
            return jnp.where(hist[level], choose(level + 1, lo_idx + span), choose(level + 1, lo_idx))

        p = pred(choose(0, 0), x)
        base = jnp.where(p, base + float(step), base)
        hist.append(p)
        step //= 2
    return jnp.where(pred(vals[n - 1], x), float(n), base)


def _peer_stats_kernel(x_ref, wqt_ref, keys_ref, r1_ref, c1_ref, rk2_ref, e2_ref, qt_ref):
    qt_ref[...] = _dot_nt(wqt_ref[...], x_ref[...].astype(BF16))
    tn = x_ref.shape[0]
    half = PEER_DKEY // 2
    kh = PEER_TOPK_HALF
    nrow = PEER_NKEYS // SUBLANES
    sub = lax.broadcasted_iota(jnp.int32, (SUBLANES, 1), 0)
    tall = lambda v: jnp.concatenate([v] * nrow, axis=0)
    split = lambda s: [s[r * SUBLANES:(r + 1) * SUBLANES, :] for r in range(nrow)]

    def spread(vals):
        out = vals[0]
        for s in range(1, SUBLANES):
            out = jnp.where(sub == s, vals[s], out)
        return out

    for h in range(PEER_HEADS):
        r0 = h * PEER_DKEY
        s1 = _dot(keys_ref[0], qt_ref[r0:r0 + half, :].astype(BF16))
        s2 = _dot(keys_ref[1], qt_ref[r0 + half:r0 + PEER_DKEY, :].astype(BF16))
        v1 = _top_sorted(split(s1))
        v2 = _top_sorted(split(s2))
        b_lo, b_hi, a_hi = spread(v2[:SUBLANES]), spread(v2[SUBLANES:]), spread(v1[SUBLANES:])
        cand = [v1[0] + b_lo, v1[0] + b_hi, v1[1] + b_lo]
        for a in range(2, SUBLANES):
            cand.append(jnp.where(sub < (PEER_TOPK + 1) // (a + 1), v1[a] + b_lo, -BIG))
        cand.append(a_hi + v2[0])
        pad = jnp.full((SUBLANES, tn), -BIG, F32)
        cv = _top_sorted(cand + [pad] * (PEER_TOPK - len(cand)))
        top, kth = cv[0], cv[PEER_TOPK - 1]
        cnt = jnp.zeros((SUBLANES, tn), F32)
        z = jnp.zeros((SUBLANES, tn), F32)
        below = jnp.full((SUBLANES, tn), -BIG, F32)
        for c in cand:
            ge = c >= kth
            cnt = cnt + jnp.where(ge, 1.0, 0.0)
            z = z + jnp.where(ge, jnp.exp(c - top), 0.0)
            below = jnp.maximum(below, jnp.where(ge, -BIG, c))
        z = _sum_sublanes(z)
        nxt = jnp.where(_sum_sublanes(cnt) > float(PEER_TOPK), kth, _max_sublanes(below))
        thr = 0.5 * (kth + nxt)
        act1 = s1 >= tall(v1[kh - 1])
        act2 = s2 >= tall(v2[kh - 1])
        need = tall(thr) - s1
        tv2 = [tall(v) for v in v2]
        r1 = _count_sorted(tv2, need, lambda v, x: v >= x)
        rk2 = _count_sorted(tv2, s2, lambda v, x: v > x)
        r1_ref[h] = _pack_pair(jnp.where(act1, r1, 0.0))
        c1_ref[h] = _pack_pair(jnp.where(act1, jnp.exp(s1 - tall(v1[0])) / tall(z), 0.0))
        rk2_ref[h] = rk2.astype(BF16)
        e2_ref[h] = jnp.where(act2, jnp.exp(s2 - tall(v2[0])), 0.0).astype(BF16)


def _peer_stats(x, w_q, sub_keys, tn=512):
    t, d = x.shape
    wqt = w_q.T.astype(BF16)
    shp = lambda dt: jax.ShapeDtypeStruct((PEER_HEADS, PEER_NKEYS, t), dt)
    spec = pl.BlockSpec((PEER_HEADS, PEER_NKEYS, tn), lambda i: (0, 0, i))
    return pl.pallas_call(
        _peer_stats_kernel,
        grid=(t // tn,),
        in_specs=[pl.BlockSpec((tn, d), lambda i: (i, 0)),
                  pl.BlockSpec((PEER_HEADS * PEER_DKEY, d), lambda i: (0, 0)),
                  pl.BlockSpec((2, PEER_NKEYS, PEER_DKEY // 2), lambda i: (0, 0, 0))],
        out_specs=[spec] * 4,
        out_shape=[shp(jnp.int32), shp(jnp.int32), shp(BF16), shp(BF16)],
        scratch_shapes=[pltpu.VMEM((PEER_HEADS * PEER_DKEY, tn), F32)],
        compiler_params=_cp("arbitrary"),
        name="peer_stats",
    )(x, wqt, sub_keys.astype(BF16))


def _peer_kernel(x_ref, u_ref, vt_ref, r1_ref, c1_ref, rk2_ref, e2_ref, g_ref, beta_ref, o_ref,
                 xb_ref, acc_ref, w_ref, *, te):
    j = pl.program_id(1)
    tn = xb_ref.shape[0]
    pack = 2 * SUBLANES
    nsub = te // PEER_NKEYS

    @pl.when(j == 0)
    def _():
        xb_ref[...] = x_ref[...].astype(BF16)
        acc_ref[...] = jnp.zeros_like(acc_ref)

    def row_bf16(ref, h, idx):
        w = jnp.broadcast_to(ref[h, pl.ds(idx, 1), :], (SUBLANES, tn))
        return jnp.concatenate([pltpu.bitcast(w, BF16)] * (PEER_NKEYS // pack), axis=0)

    act = jax.nn.gelu(_dot_nt(u_ref[...], xb_ref[...]).astype(BF16))
    for ii in range(nsub):
        idx = j * nsub + ii
        gate = jnp.zeros((PEER_NKEYS, tn), BF16)
        for h in range(PEER_HEADS):
            sel = rk2_ref[h] < row_bf16(r1_ref, h, idx)
            gate = gate + jnp.where(sel, e2_ref[h], jnp.zeros((), BF16)) * row_bf16(c1_ref, h, idx)
        sl = slice(ii * PEER_NKEYS, (ii + 1) * PEER_NKEYS)
        w_ref[sl, :] = act[sl, :] * gate
    acc_ref[...] += _dot(vt_ref[...], w_ref[...])

    @pl.when(j == pl.num_programs(1) - 1)
    def _():
        y = ALPHA * x_ref[...] + acc_ref[...].T
        o_ref[...] = _layer_norm(y, g_ref[...], beta_ref[...])


def _peer_ln(x, stats, u_tab, v_tab, g, beta, tn=512, te=512):
    t, d = x.shape
    ne = u_tab.shape[0]
    ub = u_tab.astype(BF16)
    vtb = v_tab.T.astype(BF16)
    sspec = pl.BlockSpec((PEER_HEADS, PEER_NKEYS, tn), lambda i, j: (0, 0, i))
    kern = functools.partial(_peer_kernel, te=te)
    return pl.pallas_call(
        kern,
        grid=(t // tn, ne // te),
        in_specs=[pl.BlockSpec((tn, d), lambda i, j: (i, 0)),
                  pl.BlockSpec((te, d), lambda i, j: (j, 0)),
                  pl.BlockSpec((d, te), lambda i, j: (0, j)),
                  sspec, sspec, sspec, sspec,
                  pl.BlockSpec((1, d), lambda i, j: (0, 0)), pl.BlockSpec((1, d), lambda i, j: (0, 0))],
        out_specs=pl.BlockSpec((tn, d), lambda i, j: (i, 0)),
        out_shape=jax.ShapeDtypeStruct((t, d), F32),
        scratch_shapes=[pltpu.VMEM((tn, d), BF16), pltpu.VMEM((d, tn), F32), pltpu.VMEM((te, tn), BF16)],
        compiler_params=_cp("arbitrary", "arbitrary"),
        name="peer",
    )(x, ub, vtb, *stats, g.reshape(1, -1), beta.reshape(1, -1))


def _shortconv_kernel(cb_ref, cc_ref, ch_ref, ccp_ref, chp_ref, w_ref, o_ref, ext_ref, *, ts):
    i = pl.program_id(1)
    prev = ccp_ref[...] * chp_ref[...]
    ext_ref[0:SUBLANES, :] = jnp.where(i > 0, prev, 0.0)
    ext_ref[SUBLANES:, :] = cc_ref[...] * ch_ref[...]
    acc = jnp.zeros(o_ref.shape, F32)
    for k in range(CONV_W):
        off = SUBLANES - (CONV_W - 1) + k
        acc = acc + w_ref[k:k + 1, :] * ext_ref[off:off + ts, :]
    o_ref[...] = cb_ref[...] * acc


def _shortconv(y1, conv_w, ts=512):
    bsz, seq, _ = y1.shape
    blk = lambda c: pl.BlockSpec((None, ts, CONV_CH), lambda b, i, c=c: (b, i, c))
    halo = lambda c: pl.BlockSpec((None, SUBLANES, CONV_CH),
                                  lambda b, i, c=c: (b, jnp.maximum(i * (ts // SUBLANES) - 1, 0), c))
    kern = functools.partial(_shortconv_kernel, ts=ts)
    return pl.pallas_call(
        kern,
        grid=(bsz, seq // ts),
        in_specs=[blk(0), blk(1), blk(2), halo(1), halo(2),
                  pl.BlockSpec((CONV_W, CONV_CH), lambda b, i: (0, 0))],
        out_specs=pl.BlockSpec((None, ts, CONV_CH), lambda b, i: (b, i, 0)),
        out_shape=jax.ShapeDtypeStruct((bsz, seq, CONV_CH), F32),
        scratch_shapes=[pltpu.VMEM((ts + SUBLANES, CONV_CH), F32)],
        compiler_params=_cp("arbitrary", "arbitrary"),
        name="short_conv",
    )(y1, y1, y1, y1, y1, conv_w)


def _ssdconv_kernel(u_ref, up_ref, w_ref, b_ref, o_ref, ext_ref, *, ts):
    i = pl.program_id(1)
    ext_ref[0:SUBLANES, :] = jnp.where(i > 0, up_ref[...], 0.0)
    ext_ref[SUBLANES:, :] = u_ref[...]
    acc = jnp.zeros(o_ref.shape, F32) + b_ref[...]
    for k in range(SSD_CONV_W):
        off = SUBLANES - (SSD_CONV_W - 1) + k
        acc = acc + w_ref[k:k + 1, :] * ext_ref[off:off + ts, :]
    o_ref[...] = acc * jax.nn.sigmoid(acc)


def _ssdconv(y1, w, bias, ts=512, tc=1024):
    bsz, seq, _ = y1.shape
    nch = w.shape[1]
    c0 = L1_XS // tc
    kern = functools.partial(_ssdconv_kernel, ts=ts)
    return pl.pallas_call(
        kern,
        grid=(bsz, seq // ts, nch // tc),
        in_specs=[pl.BlockSpec((None, ts, tc), lambda b, i, c: (b, i, c0 + c)),
                  pl.BlockSpec((None, SUBLANES, tc),
                               lambda b, i, c: (b, jnp.maximum(i * (ts // SUBLANES) - 1, 0), c0 + c)),
                  pl.BlockSpec((SSD_CONV_W, tc), lambda b, i, c: (0, c)),
                  pl.BlockSpec((1, tc), lambda b, i, c: (0, c))],
        out_specs=pl.BlockSpec((None, ts, tc), lambda b, i, c: (b, i, c)),
        out_shape=jax.ShapeDtypeStruct((bsz, seq, nch), F32),
        scratch_shapes=[pltpu.VMEM((ts + SUBLANES, tc), F32)],
        compiler_params=_cp("arbitrary", "arbitrary", "arbitrary"),
        name="ssd_conv",
    )(y1, y1, w, bias.reshape(1, -1))


def _ssd_kernel(xs_ref, bs_ref, cs_ref, z_ref, dt_ref, dtb_ref, aneg_ref, dskip_ref, ng_ref, sel_ref, pk_ref,
                o_ref, state_ref):
    c = pl.program_id(1)
    nb = xs_ref.shape[0]
    lc = SSD_CHUNK
    npair = SSD_REP // 2
    bs_ = range(nb)
    prs = range(npair)
    inst = [(b, pr) for b in bs_ for pr in prs]

    @pl.when(c == 0)
    def _():
        state_ref[...] = jnp.zeros_like(state_ref)

    ri = lax.broadcasted_iota(jnp.int32, (lc, lc), 0)
    ci = lax.broadcasted_iota(jnp.int32, (lc, lc), 1)
    causal = ci <= ri
    tril = jnp.where(causal, 1.0, 0.0)
    lane = lax.broadcasted_iota(jnp.int32, (lc, LANES), 1)
    lo_half = lane < SSD_HEAD_DIM
    row_lo = lax.broadcasted_iota(jnp.int32, (LANES, 1), 0) < SSD_HEAD_DIM
    sl = lambda pr: slice(pr * LANES, (pr + 1) * LANES)
    spread = lambda m: _dot(m.astype(BF16), pk_ref[...])

    dt = [jax.nn.softplus(dt_ref[b] + dtb_ref[...]) for b in bs_]
    a_cs = [jnp.dot(tril, dt[b] * aneg_ref[...], preferred_element_type=F32, precision=lax.Precision.HIGHEST)
            for b in bs_]
    a_cs_t = [a.T for a in a_cs]
    a_last = [a[lc - 1:lc, :] for a in a_cs]
    decay_s = [spread(jnp.exp(a_last[b] - a_cs[b])) for b in bs_]
    decay_o = [spread(jnp.exp(a_cs[b])) for b in bs_]
    dt_p = [spread(dt[b]) for b in bs_]
    chunk_decay = [jnp.exp(a_last[b]) for b in bs_]
    bc = [bs_ref[b].astype(BF16) for b in bs_]
    cc = [cs_ref[b].astype(BF16) for b in bs_]
    cb = [_dot_nt(cc[b], bc[b]) for b in bs_]

    heads = [(b, r) for b in bs_ for r in range(SSD_REP)]
    hi = [a.astype(BF16) for a in a_cs]
    lo = [(a_cs[b] - hi[b].astype(F32)).astype(BF16) for b in bs_]
    col = [_dot(jnp.concatenate([hi[b], lo[b]], axis=1), sel_ref[...]) for b in bs_]
    seg = {k: col[k[0]][:, k[1] * LANES:(k[1] + 1) * LANES] - a_cs_t[k[0]][k[1]:k[1] + 1, :] for k in heads}
    dec = {k: jnp.exp(jnp.where(causal, seg[k], NEG)) for k in heads}
    ms = {k: (cb[k[0]] * dec[k]).astype(BF16) for k in heads}

    x = {k: xs_ref[k[0], :, sl(k[1])] for k in inst}
    xdt = {k: x[k] * dt_p[k[0]][:, sl(k[1])] for k in inst}
    x_lo = {k: jnp.where(lo_half, xdt[k], 0.0).astype(BF16) for k in inst}
    x_hi = {k: jnp.where(lo_half, 0.0, xdt[k]).astype(BF16) for k in inst}
    y = {k: _dot(ms[k[0], 2 * k[1]], x_lo[k]) + _dot(ms[k[0], 2 * k[1] + 1], x_hi[k]) for k in inst}
    prev = {k: state_ref[k[0], sl(k[1]), :] for k in inst}
    yoff = {k: _dot_nt(cc[k[0]], prev[k].astype(BF16)) * decay_o[k[0]][:, sl(k[1])] for k in inst}
    st = {k: _dot_tn((xdt[k] * decay_s[k[0]][:, sl(k[1])]).astype(BF16), bc[k[0]]) for k in inst}
    for k in inst:
        b, pr = k
        cd = jnp.where(row_lo, chunk_decay[b][:, 2 * pr:2 * pr + 1], chunk_decay[b][:, 2 * pr + 1:2 * pr + 2])
        state_ref[b, sl(pr), :] = prev[k] * cd + st[k]
    hz = {}
    for k in inst:
        b, pr = k
        z = z_ref[b, :, sl(pr)]
        hz[k] = (y[k] + yoff[k] + dskip_ref[:, sl(pr)] * x[k]) * (z * jax.nn.sigmoid(z))
    for b in bs_:
        ssq = sum(jnp.sum(hz[b, pr] * hz[b, pr], axis=-1, keepdims=True) for pr in prs)
        inv = lax.rsqrt(ssq / (SSD_REP * SSD_HEAD_DIM) + EPS)
        for pr in prs:
            o_ref[b, :, sl(pr)] = hz[b, pr] * inv * ng_ref[:, sl(pr)]


def _ssd(y1, xbc, dt_bias_g, a_neg_g, d_skip, norm_g):
    bsz, seq, _ = y1.shape
    gw = SSD_REP * SSD_HEAD_DIM
    nchunk = seq // SSD_CHUNK
    vec = pl.BlockSpec((None, 1, LANES), lambda g, c: (g, 0, 0))
    k = np.arange(2 * LANES)[:, None] % LANES
    lanes = np.arange(SSD_REP * LANES)[None, :]
    sel = (k == lanes // LANES).astype(np.float32)
    hd = np.arange(LANES)[:, None]
    lanes = np.arange(gw)[None, :]
    pk = (hd == lanes // SSD_HEAD_DIM).astype(np.float32)
    return pl.pallas_call(
        _ssd_kernel,
        grid=(SSD_GROUPS, nchunk),
        in_specs=[
            pl.BlockSpec((bsz, SSD_CHUNK, gw), lambda g, c: (0, c, g)),
            pl.BlockSpec((bsz, SSD_CHUNK, SSD_STATE), lambda g, c: (0, c, SSD_INNER // SSD_STATE + g)),
            pl.BlockSpec((bsz, SSD_CHUNK, SSD_STATE), lambda g, c: (0, c, SSD_INNER // SSD_STATE + SSD_GROUPS + g)),
            pl.BlockSpec((bsz, SSD_CHUNK, gw), lambda g, c: (0, c, L1_Z // gw + g)),
            pl.BlockSpec((bsz, SSD_CHUNK, LANES), lambda g, c: (0, c, L1_DT // LANES + g)),
            vec, vec,
            pl.BlockSpec((1, gw), lambda g, c: (0, g)),
            pl.BlockSpec((1, gw), lambda g, c: (0, g)),
            pl.BlockSpec((2 * LANES, SSD_REP * LANES), lambda g, c: (0, 0)),
            pl.BlockSpec((LANES, gw), lambda g, c: (0, 0)),
        ],
        out_specs=pl.BlockSpec((bsz, SSD_CHUNK, gw), lambda g, c: (0, c, g)),
        out_shape=jax.ShapeDtypeStruct((bsz, seq, SSD_INNER), F32),
        scratch_shapes=[pltpu.VMEM((bsz, gw, SSD_STATE), F32)],
        compiler_params=_cp("arbitrary", "arbitrary"),
        name="ssd",
    )(xbc, xbc, xbc, y1, y1, dt_bias_g, a_neg_g, jnp.repeat(d_skip, SSD_HEAD_DIM).reshape(1, -1),
      norm_g.reshape(1, -1), jnp.asarray(sel, dtype=BF16), jnp.asarray(pk, dtype=BF16))


def _pad_cols(w, n):
    return jnp.pad(w, ((0, 0), (0, n - w.shape[1])))


def _pack_w_in0(w):
    o = np.cumsum((0, A_HEADS * HEAD_DIM, A_KV_RANK, IDX_HEADS * IDX_DIM, IDX_DIM, IDX_HEADS,
                   B_HEADS * HEAD_DIM) + (B_KV_GROUPS * HEAD_DIM,) * 6 + (B_HEADS * 3,))
    seg = lambda k: w[:, int(o[k]):int(o[k + 1])]
    gates = seg(12)
    ng = gates.shape[1] // B_KV_GROUPS
    parts = [seg(0), seg(2), seg(5)] + [seg(k) for k in range(6, 12)] + [
        seg(1), _pad_cols(seg(3), LANES), _pad_cols(seg(4), LANES)] + [
        _pad_cols(gates[:, g * ng:(g + 1) * ng], LANES) for g in range(B_KV_GROUPS)]
    return jnp.concatenate(parts, axis=1).astype(BF16)


def _pack_w_in1(w):
    dt = w[:, L1_DT:]
    parts = [w[:, :L1_DT]] + [_pad_cols(dt[:, g * SSD_REP:(g + 1) * SSD_REP], LANES) for g in range(SSD_GROUPS)]
    return jnp.concatenate(parts, axis=1).astype(BF16)


def _per_group(v):
    return jnp.pad(v.reshape(SSD_GROUPS, 1, SSD_REP), ((0, 0), (0, 0), (0, LANES - SSD_REP)))


def _peer_block(x, wq, keys, u, v, g, b):
    stats = _peer_stats(x, wq, keys)
    return _peer_ln(x, stats, u, v, g, b)


def _even_layer(x, bsz, seq, w_in, kv_gain, w_uk, w_uv, cmpk_pos, cmpk_w1, cmpk_w2, cmpv_pos, cmpv_w1, cmpv_w2,
                w_out, ln1_g, ln1_b, peer_wq, peer_keys, peer_u, peer_v, ln2_g, ln2_b):
    t = bsz * seq
    y0 = _matmul(x, _pack_w_in0(w_in)).reshape(bsz, seq, NP0)
    o_a = _dsa(y0, kv_gain, w_uk, w_uv)
    kvc = _nsa_compress(y0, jnp.stack([cmpk_pos, cmpv_pos]), jnp.stack([cmpk_w1, cmpv_w1]),
                        jnp.stack([cmpk_w2, cmpv_w2]))
    o_b = _nsa(y0, kvc)
    x = _out_ln(o_a.reshape(t, -1), o_b.reshape(t, -1), w_out, x, ln1_g, ln1_b)
    return _peer_block(x, peer_wq, peer_keys, peer_u, peer_v, ln2_g, ln2_b)


def _odd_layer(x, bsz, seq, w_in, conv_w, ssd_conv_w, ssd_conv_b, dt_bias, a_log, d_skip, norm_g,
               w_out, ln1_g, ln1_b, peer_wq, peer_keys, peer_u, peer_v, ln2_g, ln2_b):
    t = bsz * seq
    y1 = _matmul(x, _pack_w_in1(w_in)).reshape(bsz, seq, NP1)
    y_c = _shortconv(y1, conv_w)
    xbc = _ssdconv(y1, ssd_conv_w, ssd_conv_b)
    y_d = _ssd(y1, xbc, _per_group(dt_bias), _per_group(-jnp.exp(a_log)), d_skip, norm_g)
    x = _out_ln(y_c.reshape(t, -1), y_d.reshape(t, -1), w_out, x, ln1_g, ln1_b)
    return _peer_block(x, peer_wq, peer_keys, peer_u, peer_v, ln2_g, ln2_b)


def kernel(x, l0_w_in, l0_kv_gain, l0_w_uk, l0_w_uv, l0_cmpk_pos, l0_cmpk_w1, l0_cmpk_w2, l0_cmpv_pos, l0_cmpv_w1, l0_cmpv_w2, l0_w_out, l0_ln1_g, l0_ln1_b, l0_peer_wq, l0_peer_keys, l0_peer_u, l0_peer_v, l0_ln2_g, l0_ln2_b, l1_w_in, l1_conv_w, l1_ssd_conv_w, l1_ssd_conv_b, l1_dt_bias, l1_a_log, l1_d_skip, l1_norm_g, l1_w_out, l1_ln1_g, l1_ln1_b, l1_peer_wq, l1_peer_keys, l1_peer_u, l1_peer_v, l1_ln2_g, l1_ln2_b):
    bsz, seq, d = x.shape
    h = x.reshape(bsz * seq, d)
    h = _even_layer(h, bsz, seq, l0_w_in, l0_kv_gain, l0_w_uk, l0_w_uv, l0_cmpk_pos, l0_cmpk_w1, l0_cmpk_w2,
                    l0_cmpv_pos, l0_cmpv_w1, l0_cmpv_w2, l0_w_out, l0_ln1_g, l0_ln1_b,
                    l0_peer_wq, l0_peer_keys, l0_peer_u, l0_peer_v, l0_ln2_g, l0_ln2_b)
    h = _odd_layer(h, bsz, seq, l1_w_in, l1_conv_w, l1_ssd_conv_w, l1_ssd_conv_b, l1_dt_bias, l1_a_log,
                   l1_d_skip, l1_norm_g, l1_w_out, l1_ln1_g, l1_ln1_b,
                   l1_peer_wq, l1_peer_keys, l1_peer_u, l1_peer_v, l1_ln2_g, l1_ln2_b)
    return h.reshape(bsz, seq, d)
```

```python
import functools

import numpy as np
import jax
import jax.numpy as jnp
from jax import lax
from jax.experimental import pallas as pl
from jax.experimental.pallas import tpu as pltpu

F32 = jnp.float32
BF16 = jnp.bfloat16

D_MODEL = 2048
HEAD_DIM = 128
Q_BLOCK = 128
NEG = -1e30
FORCE = 1e30
EPS = 1e-5

A_HEADS = 8
A_KV_RANK = 256
IDX_HEADS = 16
IDX_DIM = 64
DSA_TOPK_MAX = 256
DSA_KEY_CHUNK = 512
DSA_SUM_ROWS = 16

B_HEADS = 8
B_KV_GROUPS = 2
NSA_CMP_LEN = 32
NSA_CMP_STRIDE = 16
NSA_SLC_LEN = 64
NSA_SLC_N = 16
NSA_WINDOW = 512
NSA_KEY_CHUNK = 512
NSA_SUM_ROWS = 16

CONV_CH = 1024
CONV_W = 3

SSD_HEAD_DIM = 64
SSD_HEADS = 48
SSD_INNER = SSD_HEADS * SSD_HEAD_DIM
SSD_GROUPS = 8
SSD_STATE = 128
SSD_CONV_W = 4
SSD_CHUNK = 128
SSD_REP = SSD_HEADS // SSD_GROUPS

PEER_HEADS = 8
PEER_NKEYS = 128
PEER_N = PEER_NKEYS * PEER_NKEYS
PEER_DKEY = 256
PEER_TOPK_HALF = 16
PEER_TOPK = 16

DEPTH = 2
ALPHA = (2 * DEPTH) ** 0.25

LANES = 128
SUBLANES = 8
VMEM_LIMIT = 56 * 1024 * 1024

L0_Q, L0_IQ, L0_BQ, L0_KV, L0_CKV, L0_IK, L0_IW, L0_GATE = 0, 1024, 2048, 3072, 4608, 4864, 4992, 5120
NP0 = 5376
L1_Z, L1_XS, L1_BS, L1_CS, L1_DT = 3072, 6144, 9216, 10240, 11264
NP1 = L1_DT + SSD_GROUPS * LANES


def _cp(*sem):
    return pltpu.CompilerParams(dimension_semantics=sem, vmem_limit_bytes=VMEM_LIMIT)


def _dot(a, b):
    return jnp.dot(a, b, preferred_element_type=F32)


def _dot_nt(a, b):
    return lax.dot_general(a, b, (((1,), (1,)), ((), ())), preferred_element_type=F32)


def _dot_tn(a, b):
    return lax.dot_general(a, b, (((0,), (0,)), ((), ())), preferred_element_type=F32)


def _layer_norm(v, g, b):
    mu = jnp.mean(v, axis=-1, keepdims=True)
    d = v - mu
    var = jnp.mean(d * d, axis=-1, keepdims=True)
    return d * lax.rsqrt(var + EPS) * g + b


def _mm_kernel(x_ref, w_ref, o_ref, xb_ref):
    @pl.when(pl.program_id(1) == 0)
    def _():
        xb_ref[...] = x_ref[...].astype(BF16)

    o_ref[...] = _dot(xb_ref[...], w_ref[...])


def _matmul(x, w, tm=1024, tn=768):
    t, k = x.shape
    n = w.shape[1]
    return pl.pallas_call(
        _mm_kernel,
        grid=(t // tm, n // tn),
        in_specs=[pl.BlockSpec((tm, k), lambda i, j: (i, 0)),
                  pl.BlockSpec((k, tn), lambda i, j: (0, j))],
        out_specs=pl.BlockSpec((tm, tn), lambda i, j: (i, j)),
        out_shape=jax.ShapeDtypeStruct((t, n), F32),
        scratch_shapes=[pltpu.VMEM((tm, k), BF16)],
        compiler_params=_cp("arbitrary", "arbitrary"),
        name="in_proj",
    )(x, w)


def _dsa_kernel(q_ref, iq_ref, iw_ref, ckv_ref, ik_ref, gain_ref, wuk_ref, wuv_ref, o_ref,
                ckvn_ref, ckvt_ref, iklo_ref, ikhi_ref, key_ref, bias_ref, qlt_ref, m_ref, acc_ref,
                *, seq, topk):
    i = pl.program_id(1)

    @pl.when(i == 0)
    def _():
        c = ckv_ref[...]
        ms = jnp.mean(c * c, axis=-1, keepdims=True)
        cn = c * lax.rsqrt(ms + EPS) * gain_ref[...]
        ckvn_ref[...] = cn.astype(BF16)
        ckvt_ref[0:A_KV_RANK, :] = cn.T.astype(BF16)
        ckvt_ref[A_KV_RANK:, :] = jnp.ones((DSA_SUM_ROWS, seq), BF16)
        ik = ik_ref[...]
        iklo_ref[...] = ik.astype(BF16)
        ikhi_ref[...] = pltpu.roll(ik, IDX_DIM, axis=1).astype(BF16)

    kc = DSA_KEY_CHUNK
    nch = (i * Q_BLOCK + Q_BLOCK + kc - 1) // kc
    qpos = i * Q_BLOCK + lax.broadcasted_iota(jnp.int32, (kc, Q_BLOCK), 1)
    kiota = lax.broadcasted_iota(jnp.int32, (kc, Q_BLOCK), 0)

    def chunk(c):
        return pl.ds(pl.multiple_of(c * kc, kc), kc)

    iwt = iw_ref[...].T * ((IDX_DIM * IDX_HEADS) ** -0.5)
    npair = IDX_HEADS // 2
    iq2 = [jnp.concatenate([iq_ref[:, p * LANES:(p + 1) * LANES], iq_ref[:, (p + 1) * LANES:(p + 2) * LANES]],
                           axis=0).astype(BF16) for p in range(0, npair, 2)]

    def index_chunk(c, carry):
        lo = iklo_ref[chunk(c), :]
        hi = ikhi_ref[chunk(c), :]
        score = jnp.zeros((kc, Q_BLOCK), F32)
        for t, iqt in enumerate(iq2):
            d_lo = jnp.maximum(_dot_nt(lo, iqt), 0.0)
            d_hi = jnp.maximum(_dot_nt(hi, iqt), 0.0)
            for u in range(2):
                h0 = 4 * t + 2 * u
                score = score + d_lo[:, u * Q_BLOCK:(u + 1) * Q_BLOCK] * iwt[h0:h0 + 1, :]
                score = score + d_hi[:, u * Q_BLOCK:(u + 1) * Q_BLOCK] * iwt[h0 + 1:h0 + 2, :]
        score = jnp.where(c * kc + kiota <= qpos, score, NEG)
        bits = lax.bitcast_convert_type(score, jnp.int32)
        key_ref[chunk(c), :] = jnp.where(bits < 0, bits ^ jnp.int32(0x7FFFFFFF), bits)
        return carry

    lax.fori_loop(0, nch, index_chunk, 0)

    def count_ge(cand):
        def body(c, acc):
            ge = jnp.where(key_ref[chunk(c), :] >= cand, 1.0, 0.0)
            while ge.shape[0] > SUBLANES:
                half = ge.shape[0] // 2
                ge = ge[:half] + ge[half:]
            return acc + ge
        acc = lax.fori_loop(0, nch, body, jnp.zeros((SUBLANES, Q_BLOCK), F32))
        return jnp.sum(acc, axis=0, keepdims=True)

    kf = float(topk)
    zero = jnp.zeros((1, Q_BLOCK), jnp.int32)
    prefix = jnp.where(count_ge(zero) >= kf, zero, jnp.int32(-2 ** 31))

    def bit_step(t, prefix):
        cand = prefix + lax.shift_left(jnp.int32(1), 30 - t)
        return jnp.where(count_ge(cand) >= kf, cand, prefix)

    prefix = lax.fori_loop(0, 31, bit_step, prefix)

    def bias_chunk(c, carry):
        keep = (key_ref[chunk(c), :] >= prefix) & (c * kc + kiota <= qpos)
        bias_ref[chunk(c), :] = jnp.where(keep, 0.0, NEG).astype(BF16)
        return carry

    lax.fori_loop(0, nch, bias_chunk, 0)

    scale = HEAD_DIM ** -0.5
    for h in range(A_HEADS):
        qh = q_ref[:, h * HEAD_DIM:(h + 1) * HEAD_DIM].astype(BF16)
        qlt_ref[:, h * Q_BLOCK:(h + 1) * Q_BLOCK] = (_dot_nt(wuk_ref[h], qh) * scale).astype(BF16)
    m_ref[...] = jnp.full(m_ref.shape, -jnp.inf, F32)
    acc_ref[...] = jnp.zeros_like(acc_ref)

    def attend(c, carry):
        s = _dot(ckvn_ref[chunk(c), :], qlt_ref[...]).astype(BF16)
        s = s + jnp.concatenate([bias_ref[chunk(c), :]] * A_HEADS, axis=1)
        m_old = m_ref[...]
        m_new = jnp.maximum(m_old, jnp.max(s, axis=0, keepdims=True).astype(F32))
        p = jnp.exp(s - m_new.astype(BF16))
        alpha = jnp.exp(m_old - m_new)
        acc_ref[...] = alpha * acc_ref[...] + _dot(ckvt_ref[:, chunk(c)], p)
        m_ref[...] = m_new
        return carry

    lax.fori_loop(0, nch, attend, 0)
    olt = (acc_ref[0:A_KV_RANK, :] / acc_ref[A_KV_RANK:A_KV_RANK + 1, :]).astype(BF16)
    for h in range(A_HEADS):
        o_ref[:, h * HEAD_DIM:(h + 1) * HEAD_DIM] = _dot_tn(olt[:, h * Q_BLOCK:(h + 1) * Q_BLOCK], wuv_ref[h])


def _dsa(y0, kv_gain, w_uk, w_uv):
    bsz, seq, _ = y0.shape
    topk = min(DSA_TOPK_MAX, seq // 4)
    nb = seq // Q_BLOCK
    nh = A_HEADS * Q_BLOCK
    kern = functools.partial(_dsa_kernel, seq=seq, topk=topk)
    return pl.pallas_call(
        kern,
        grid=(bsz, nb),
        in_specs=[
            pl.BlockSpec((None, Q_BLOCK, 1024), lambda b, i: (b, i, L0_Q // 1024)),
            pl.BlockSpec((None, Q_BLOCK, 1024), lambda b, i: (b, i, L0_IQ // 1024)),
            pl.BlockSpec((None, Q_BLOCK, LANES), lambda b, i: (b, i, L0_IW // LANES)),
            pl.BlockSpec((None, seq, A_KV_RANK), lambda b, i: (b, 0, L0_CKV // A_KV_RANK)),
            pl.BlockSpec((None, seq, LANES), lambda b, i: (b, 0, L0_IK // LANES)),
            pl.BlockSpec((1, A_KV_RANK), lambda b, i: (0, 0)),
            pl.BlockSpec((A_HEADS, A_KV_RANK, HEAD_DIM), lambda b, i: (0, 0, 0)),
            pl.BlockSpec((A_HEADS, A_KV_RANK, HEAD_DIM), lambda b, i: (0, 0, 0)),
        ],
        out_specs=pl.BlockSpec((None, Q_BLOCK, A_HEADS * HEAD_DIM), lambda b, i: (b, i, 0)),
        out_shape=jax.ShapeDtypeStruct((bsz, seq, A_HEADS * HEAD_DIM), F32),
        scratch_shapes=[pltpu.VMEM((seq, A_KV_RANK), BF16), pltpu.VMEM((A_KV_RANK + DSA_SUM_ROWS, seq), BF16),
                        pltpu.VMEM((seq, LANES), BF16), pltpu.VMEM((seq, LANES), BF16),
                        pltpu.VMEM((seq, Q_BLOCK), jnp.int32), pltpu.VMEM((seq, Q_BLOCK), BF16),
                        pltpu.VMEM((A_KV_RANK, nh), BF16),
                        pltpu.VMEM((1, nh), F32),
                        pltpu.VMEM((A_KV_RANK + DSA_SUM_ROWS, nh), F32)],
        compiler_params=_cp("arbitrary", "arbitrary"),
        name="dsa",
    )(y0, y0, y0, y0, y0, kv_gain.reshape(1, -1), w_uk.astype(BF16), w_uv.astype(BF16))


def _nsa_cmp_kernel(x_ref, pos_ref, w1_ref, w2_ref, o_ref, *, nblk):
    half = NSA_CMP_LEN // 2
    a = jnp.zeros((nblk, HEAD_DIM), F32)
    b = jnp.zeros((nblk, HEAD_DIM), F32)
    for l in range(half):
        rows = x_ref[pl.ds(l, nblk, stride=NSA_CMP_STRIDE), :]
        a = a + _dot((rows + pos_ref[l:l + 1, :]).astype(BF16), w1_ref[l].astype(BF16))
        b = b + _dot((rows + pos_ref[l + half:l + half + 1, :]).astype(BF16),
                     w1_ref[l + half].astype(BF16))
    h = jax.nn.gelu(a + pltpu.roll(b, nblk - 1, axis=0))
    o_ref[...] = _dot(h.astype(BF16), w2_ref[...].astype(BF16))


def _nsa_compress(y0, pos, w1, w2):
    bsz, seq, _ = y0.shape
    nblk = seq // NSA_CMP_STRIDE
    kern = functools.partial(_nsa_cmp_kernel, nblk=nblk)
    return pl.pallas_call(
        kern,
        grid=(bsz, 2, B_KV_GROUPS),
        in_specs=[
            pl.BlockSpec((None, seq, HEAD_DIM), lambda b, t, g: (b, 0, L0_KV // HEAD_DIM + 2 * t + g)),
            pl.BlockSpec((None, NSA_CMP_LEN, HEAD_DIM), lambda b, t, g: (t, 0, 0)),
            pl.BlockSpec((None, NSA_CMP_LEN, HEAD_DIM, HEAD_DIM), lambda b, t, g: (t, 0, 0, 0)),
            pl.BlockSpec((None, HEAD_DIM, HEAD_DIM), lambda b, t, g: (t, 0, 0)),
        ],
        out_specs=pl.BlockSpec((None, None, None, nblk, HEAD_DIM), lambda b, t, g: (b, t, g, 0, 0)),
        out_shape=jax.ShapeDtypeStruct((bsz, 2, B_KV_GROUPS, nblk, HEAD_DIM), F32),
        compiler_params=_cp("arbitrary", "arbitrary", "arbitrary"),
        name="nsa_compress",
    )(y0, pos, w1, w2)


def _nsa_kernel(q_ref, gate_ref, kc_ref, vc_ref, ks_ref, vs_ref, kw_ref, vw_ref, ovl_ref, exp_ref,
                o_ref, ksb_ref, vst_ref, kwb_ref, vwt_ref, kcb_ref, vct_ref, m_ref, acc_ref, *, seq, nsel):
    i = pl.program_id(2)
    hpg = B_HEADS // B_KV_GROUPS
    ncmp = kc_ref.shape[0]
    wlen = NSA_WINDOW + Q_BLOCK
    d = HEAD_DIM

    @pl.when(i == 0)
    def _():
        ones = lambda n: jnp.ones((NSA_SUM_ROWS, n), BF16)
        ksb_ref[...] = ks_ref[...].astype(BF16)
        kwb_ref[...] = kw_ref[...].astype(BF16)
        kcb_ref[...] = kc_ref[...].astype(BF16)
        vst_ref[0:d, :] = vs_ref[...].T.astype(BF16)
        vst_ref[d:, :] = ones(seq)
        vwt_ref[0:d, :] = vw_ref[...].T.astype(BF16)
        vwt_ref[d:, :] = ones(seq)
        vct_ref[0:d, :] = vc_ref[...].T.astype(BF16)
        vct_ref[d:, :] = ones(ncmp)

    start = i * Q_BLOCK
    scale = HEAD_DIM ** -0.5
    qt = jnp.concatenate([(q_ref[:, h * d:(h + 1) * d] * scale).T for h in range(hpg)], axis=1).astype(BF16)
    qlane = start + lax.broadcasted_iota(jnp.int32, (1, Q_BLOCK), 1)
    qlane4 = jnp.concatenate([qlane] * hpg, axis=1)

    def softmax_pv(s, vt):
        m = jnp.max(s, axis=0, keepdims=True)
        ext = _dot(vt, jnp.exp(s - m))
        return ext[0:d, :] / ext[d:d + 1, :]

    s_c = _dot(kcb_ref[...], qt)
    cend = lax.broadcasted_iota(jnp.int32, (ncmp, 1), 0) * NSA_CMP_STRIDE + (NSA_CMP_LEN - 1)
    s_c = jnp.where(cend <= qlane4, s_c, NEG)
    p_c = jnp.exp(s_c - jnp.max(s_c, axis=0, keepdims=True))
    p_c = p_c / jnp.sum(p_c, axis=0, keepdims=True)
    p_c = jnp.where(qlane4 >= NSA_CMP_LEN - 1, p_c, 0.0)
    o_c = _dot(vct_ref[0:d, :], p_c.astype(BF16))

    psum = p_c[:, 0:Q_BLOCK]
    for h in range(1, hpg):
        psum = psum + p_c[:, h * Q_BLOCK:(h + 1) * Q_BLOCK]
    imp = jnp.dot(ovl_ref[...], psum, preferred_element_type=F32, precision=lax.Precision.HIGHEST)
    nslc = seq // NSA_SLC_LEN
    jj = lax.broadcasted_iota(jnp.int32, (LANES, 1), 0)
    cur = qlane // NSA_SLC_LEN
    forced = (jj == 0) | (jj == cur) | (jj == cur - 1)
    imp = jnp.where(forced, FORCE, imp)
    imp = jnp.where(jj <= cur, imp, NEG)
    imp = jnp.where(jj < nslc, imp, -3e38)
    rank = jnp.zeros((LANES, Q_BLOCK), F32)
    for t in range(nslc):
        row = imp[t:t + 1, :]
        later = jnp.where(jj > t, 1.0, 0.0)
        rank = rank + jnp.where(row > imp, 1.0, jnp.where(row == imp, later, 0.0))
    selb = jnp.where(rank < float(nsel), 1.0, 0.0).astype(BF16)

    kc = NSA_KEY_CHUNK
    nch = (start + Q_BLOCK + kc - 1) // kc
    kiota = lax.broadcasted_iota(jnp.int32, (kc, 1), 0)
    m_ref[...] = jnp.full(m_ref.shape, -jnp.inf, F32)
    acc_ref[...] = jnp.zeros_like(acc_ref)

    def attend(c, carry):
        ck = pl.ds(pl.multiple_of(c * kc, kc), kc)
        keysel = _dot(exp_ref[ck, :], selb)
        keep = (keysel > 0.5) & (c * kc + kiota <= qlane)
        bias = jnp.where(keep, 0.0, NEG).astype(BF16)
        s = _dot(ksb_ref[ck, :], qt).astype(BF16) + jnp.concatenate([bias] * hpg, axis=1)
        m_old = m_ref[...]
        m_new = jnp.maximum(m_old, jnp.max(s, axis=0, keepdims=True).astype(F32))
        p = jnp.exp(s - m_new.astype(BF16))
        acc_ref[...] = jnp.exp(m_old - m_new) * acc_ref[...] + _dot(vst_ref[:, ck], p)
        m_ref[...] = m_new
        return carry

    lax.fori_loop(0, nch, attend, 0)
    o_s = acc_ref[0:d, :] / acc_ref[d:d + 1, :]

    ws = pl.multiple_of(jnp.maximum(start - NSA_WINDOW, 0), Q_BLOCK)
    kposw = ws + lax.broadcasted_iota(jnp.int32, (wlen, 1), 0)
    vis_w = (kposw <= qlane4) & (kposw > qlane4 - NSA_WINDOW)
    s_w = _dot(kwb_ref[pl.ds(ws, wlen), :], qt).astype(BF16)
    o_w = softmax_pv(jnp.where(vis_w, s_w, jnp.asarray(NEG, BF16)), vwt_ref[:, pl.ds(ws, wlen)])

    gate = jax.nn.sigmoid(gate_ref[...]).T
    for h in range(hpg):
        cs = slice(h * Q_BLOCK, (h + 1) * Q_BLOCK)
        o = (gate[3 * h:3 * h + 1, :] * o_c[:, cs] + gate[3 * h + 1:3 * h + 2, :] * o_s[:, cs]
             + gate[3 * h + 2:3 * h + 3, :] * o_w[:, cs])
        o_ref[:, h * d:(h + 1) * d] = o.T


def _nsa(y0, kvc):
    bsz, seq, _ = y0.shape
    nb = seq // Q_BLOCK
    nslc = seq // NSA_SLC_LEN
    nsel = min(NSA_SLC_N, nslc)
    ncmp = seq // NSA_CMP_STRIDE
    hpg = B_HEADS // B_KV_GROUPS
    cs = np.arange(ncmp)[None, :] * NSA_CMP_STRIDE
    ss = np.arange(LANES)[:, None] * NSA_SLC_LEN
    ovl = ((cs < ss + NSA_SLC_LEN) & (cs + NSA_CMP_LEN > ss) & (np.arange(ncmp)[None, :] < ncmp - 1)
           & (np.arange(LANES)[:, None] < nslc)).astype(np.float32)
    expand = (np.arange(seq)[:, None] // NSA_SLC_LEN == np.arange(LANES)[None, :]).astype(np.float32)
    kvb = L0_KV // HEAD_DIM
    full = lambda off: pl.BlockSpec((None, seq, HEAD_DIM), lambda b, g, i, off=off: (b, 0, kvb + off + g))
    kern = functools.partial(_nsa_kernel, seq=seq, nsel=nsel)
    dx = HEAD_DIM + NSA_SUM_ROWS
    return pl.pallas_call(
        kern,
        grid=(bsz, B_KV_GROUPS, nb),
        in_specs=[
            pl.BlockSpec((None, Q_BLOCK, hpg * HEAD_DIM), lambda b, g, i: (b, i, L0_BQ // (hpg * HEAD_DIM) + g)),
            pl.BlockSpec((None, Q_BLOCK, LANES), lambda b, g, i: (b, i, L0_GATE // LANES + g)),
            pl.BlockSpec((None, None, None, ncmp, HEAD_DIM), lambda b, g, i: (b, 0, g, 0, 0)),
            pl.BlockSpec((None, None, None, ncmp, HEAD_DIM), lambda b, g, i: (b, 1, g, 0, 0)),
            full(4), full(6), full(8), full(10),
            pl.BlockSpec((LANES, ncmp), lambda b, g, i: (0, 0)),
            pl.BlockSpec((seq, LANES), lambda b, g, i: (0, 0)),
        ],
        out_specs=pl.BlockSpec((None, Q_BLOCK, hpg * HEAD_DIM), lambda b, g, i: (b, i, g)),
        out_shape=jax.ShapeDtypeStruct((bsz, seq, B_HEADS * HEAD_DIM), F32),
        scratch_shapes=[pltpu.VMEM((seq, HEAD_DIM), BF16), pltpu.VMEM((dx, seq), BF16),
                        pltpu.VMEM((seq, HEAD_DIM), BF16), pltpu.VMEM((dx, seq), BF16),
                        pltpu.VMEM((ncmp, HEAD_DIM), BF16), pltpu.VMEM((dx, ncmp), BF16),
                        pltpu.VMEM((1, hpg * Q_BLOCK), F32), pltpu.VMEM((dx, hpg * Q_BLOCK), F32)],
        compiler_params=_cp("arbitrary", "arbitrary", "arbitrary"),
        name="nsa",
    )(y0, y0, kvc, kvc, y0, y0, y0, y0, jnp.asarray(ovl), jnp.asarray(expand, dtype=BF16))


def _out_ln_kernel(a_ref, b_ref, wa_ref, wb_ref, x_ref, g_ref, beta_ref, o_ref):
    acc = _dot(a_ref[...].astype(BF16), wa_ref[...]) + _dot(b_ref[...].astype(BF16), wb_ref[...])
    o_ref[...] = _layer_norm(ALPHA * x_ref[...] + acc, g_ref[...], beta_ref[...])


def _out_ln(a, b, w_out, x, g, beta, tm=256):
    t, ka = a.shape
    kb = b.shape[1]
    d = x.shape[1]
    wa = w_out[:ka].astype(BF16)
    wb = w_out[ka:].astype(BF16)
    return pl.pallas_call(
        _out_ln_kernel,
        grid=(t // tm,),
        in_specs=[pl.BlockSpec((tm, ka), lambda i: (i, 0)), pl.BlockSpec((tm, kb), lambda i: (i, 0)),
                  pl.BlockSpec((ka, d), lambda i: (0, 0)), pl.BlockSpec((kb, d), lambda i: (0, 0)),
                  pl.BlockSpec((tm, d), lambda i: (i, 0)),
                  pl.BlockSpec((1, d), lambda i: (0, 0)), pl.BlockSpec((1, d), lambda i: (0, 0))],
        out_specs=pl.BlockSpec((tm, d), lambda i: (i, 0)),
        out_shape=jax.ShapeDtypeStruct((t, d), F32),
        compiler_params=_cp("arbitrary"),
        name="out_proj_ln",
    )(a, b, wa, wb, x, g.reshape(1, -1), beta.reshape(1, -1))


BIG = 1e30


def _pack_pair(v):
    b = lax.bitcast_convert_type(v.astype(BF16).astype(F32), jnp.int32)
    return b | lax.shift_right_logical(b, 16)


def _cmp_exchange(rows, i, l):
    a, b = rows[i], rows[l]
    rows[i], rows[l] = jnp.maximum(a, b), jnp.minimum(a, b)


def _bitonic_merge_desc(rows):
    n = len(rows)
    j = n // 2
    while j >= 1:
        for i in range(n):
            if i ^ j > i:
                _cmp_exchange(rows, i, i ^ j)
        j //= 2
    return rows


def _bitonic_sort_desc(rows):
    n = len(rows)
    k = 2
    while k <= n:
        j = k // 2
        while j >= 1:
            for i in range(n):
                l = i ^ j
                if l > i:
                    if (i & k) == 0:
                        _cmp_exchange(rows, i, l)
                    else:
                        _cmp_exchange(rows, l, i)
            j //= 2
        k *= 2
    return rows


def _top_sorted(rows):
    n = len(rows)
    rows = _bitonic_sort_desc(list(rows))
    shift = SUBLANES // 2
    while shift >= 1:
        other = [pltpu.roll(r, shift, axis=0) for r in rows]
        rows = _bitonic_merge_desc([jnp.maximum(rows[k], other[n - 1 - k]) for k in range(n)])
        shift //= 2
    return rows


def _sum_sublanes(x):
    shift = SUBLANES // 2
    while shift >= 1:
        x = x + pltpu.roll(x, shift, axis=0)
        shift //= 2
    return x


def _max_sublanes(x):
    shift = SUBLANES // 2
    while shift >= 1:
        x = jnp.maximum(x, pltpu.roll(x, shift, axis=0))
        shift //= 2
    return x


def _count_sorted(vals, x, pred):
    n = len(vals)
    step = n // 2
    base = jnp.zeros_like(x)
    hist = []
    while step >= 1:
        options = [vals[b + step - 1] for b in range(0, n, 2 * step)]

        def choose(level, lo_idx):
            if level == len(hist):
                return options[lo_idx]
            span = len(options) >> (level + 1)
            return jnp.where(hist[level], choose(level + 1, lo_idx + span), choose(level + 1, lo_idx))

        p = pred(choose(0, 0), x)
        base = jnp.where(p, base + float(step), base)
        hist.append(p)
        step //= 2
    return jnp.where(pred(vals[n - 1], x), float(n), base)


def _peer_stats_kernel(x_ref, wqt_ref, keys_ref, r1_ref, c1_ref, rk2_ref, e2_ref, qt_ref):
    qt_ref[...] = _dot_nt(wqt_ref[...], x_ref[...].astype(BF16))
    tn = x_ref.shape[0]
    half = PEER_DKEY // 2
    kh = PEER_TOPK_HALF
    nrow = PEER_NKEYS // SUBLANES
    sub = lax.broadcasted_iota(jnp.int32, (SUBLANES, 1), 0)
    tall = lambda v: jnp.concatenate([v] * nrow, axis=0)
    split = lambda s: [s[r * SUBLANES:(r + 1) * SUBLANES, :] for r in range(nrow)]

    def spread(vals):
        out = vals[0]
        for s in range(1, SUBLANES):
            out = jnp.where(sub == s, vals[s], out)
        return out

    for h in range(PEER_HEADS):
        r0 = h * PEER_DKEY
        s1 = _dot(keys_ref[0], qt_ref[r0:r0 + half, :].astype(BF16))
        s2 = _dot(keys_ref[1], qt_ref[r0 + half:r0 + PEER_DKEY, :].astype(BF16))
        v1 = _top_sorted(split(s1))
        v2 = _top_sorted(split(s2))
        b_lo, b_hi, a_hi = spread(v2[:SUBLANES]), spread(v2[SUBLANES:]), spread(v1[SUBLANES:])
        cand = [v1[0] + b_lo, v1[0] + b_hi, v1[1] + b_lo]
        for a in range(2, SUBLANES):
            cand.append(jnp.where(sub < (PEER_TOPK + 1) // (a + 1), v1[a] + b_lo, -BIG))
        cand.append(a_hi + v2[0])
        pad = jnp.full((SUBLANES, tn), -BIG, F32)
        cv = _top_sorted(cand + [pad] * (PEER_TOPK - len(cand)))
        top, kth = cv[0], cv[PEER_TOPK - 1]
        cnt = jnp.zeros((SUBLANES, tn), F32)
        z = jnp.zeros((SUBLANES, tn), F32)
        below = jnp.full((SUBLANES, tn), -BIG, F32)
        for c in cand:
            ge = c >= kth
            cnt = cnt + jnp.where(ge, 1.0, 0.0)
            z = z + jnp.where(ge, jnp.exp(c - top), 0.0)
            below = jnp.maximum(below, jnp.where(ge, -BIG, c))
        z = _sum_sublanes(z)
        nxt = jnp.where(_sum_sublanes(cnt) > float(PEER_TOPK), kth, _max_sublanes(below))
        thr = 0.5 * (kth + nxt)
        act1 = s1 >= tall(v1[kh - 1])
        act2 = s2 >= tall(v2[kh - 1])
        need = tall(thr) - s1
        tv2 = [tall(v) for v in v2]
        r1 = _count_sorted(tv2, need, lambda v, x: v >= x)
        rk2 = _count_sorted(tv2, s2, lambda v, x: v > x)
        r1_ref[h] = _pack_pair(jnp.where(act1, r1, 0.0))
        c1_ref[h] = _pack_pair(jnp.where(act1, jnp.exp(s1 - tall(v1[0])) / tall(z), 0.0))
        rk2_ref[h] = rk2.astype(BF16)
        e2_ref[h] = jnp.where(act2, jnp.exp(s2 - tall(v2[0])), 0.0).astype(BF16)


def _peer_stats(x, w_q, sub_keys, tn=512):
    t, d = x.shape
    wqt = w_q.T.astype(BF16)
    shp = lambda dt: jax.ShapeDtypeStruct((PEER_HEADS, PEER_NKEYS, t), dt)
    spec = pl.BlockSpec((PEER_HEADS, PEER_NKEYS, tn), lambda i: (0, 0, i))
    return pl.pallas_call(
        _peer_stats_kernel,
        grid=(t // tn,),
        in_specs=[pl.BlockSpec((tn, d), lambda i: (i, 0)),
                  pl.BlockSpec((PEER_HEADS * PEER_DKEY, d), lambda i: (0, 0)),
                  pl.BlockSpec((2, PEER_NKEYS, PEER_DKEY // 2), lambda i: (0, 0, 0))],
        out_specs=[spec] * 4,
        out_shape=[shp(jnp.int32), shp(jnp.int32), shp(BF16), shp(BF16)],
        scratch_shapes=[pltpu.VMEM((PEER_HEADS * PEER_DKEY, tn), F32)],
        compiler_params=_cp("arbitrary"),
        name="peer_stats",
    )(x, wqt, sub_keys.astype(BF16))


def _peer_kernel(x_ref, u_ref, v_ref, r1_ref, c1_ref, rk2_ref, e2_ref, g_ref, beta_ref, o_ref,
                 xb_ref, acc_ref, w_ref, *, te):
    j = pl.program_id(1)
    tn = xb_ref.shape[0]
    pack = 2 * SUBLANES
    nsub = te // PEER_NKEYS

    @pl.when(j == 0)
    def _():
        xb_ref[...] = x_ref[...].astype(BF16)
        acc_ref[...] = jnp.zeros_like(acc_ref)

    def row_bf16(ref, h, idx):
        w = jnp.broadcast_to(ref[h, pl.ds(idx, 1), :], (SUBLANES, tn))
        return jnp.concatenate([pltpu.bitcast(w, BF16)] * (PEER_NKEYS // pack), axis=0)

    act = jax.nn.gelu(_dot_nt(u_ref[...], xb_ref[...]).astype(BF16))
    for ii in range(nsub):
        idx = j * nsub + ii
        gate = jnp.zeros((PEER_NKEYS, tn), BF16)
        for h in range(PEER_HEADS):
            sel = rk2_ref[h] < row_bf16(r1_ref, h, idx)
            gate = gate + jnp.where(sel, e2_ref[h], jnp.zeros((), BF16)) * row_bf16(c1_ref, h, idx)
        sl = slice(ii * PEER_NKEYS, (ii + 1) * PEER_NKEYS)
        w_ref[sl, :] = act[sl, :] * gate
    acc_ref[...] += _dot_tn(w_ref[...], v_ref[...])

    @pl.when(j == pl.num_programs(1) - 1)
    def _():
        o_ref[...] = _layer_norm(ALPHA * x_ref[...] + acc_ref[...], g_ref[...], beta_ref[...])


def _peer_ln(x, stats, u_tab, v_tab, g, beta, tn=512, te=512):
    t, d = x.shape
    ne = u_tab.shape[0]
    ub = u_tab.astype(BF16)
    vb = v_tab.astype(BF16)
    sspec = pl.BlockSpec((PEER_HEADS, PEER_NKEYS, tn), lambda i, j: (0, 0, i))
    kern = functools.partial(_peer_kernel, te=te)
    return pl.pallas_call(
        kern,
        grid=(t // tn, ne // te),
        in_specs=[pl.BlockSpec((tn, d), lambda i, j: (i, 0)),
                  pl.BlockSpec((te, d), lambda i, j: (j, 0)),
                  pl.BlockSpec((te, d), lambda i, j: (j, 0)),
                  sspec, sspec, sspec, sspec,
                  pl.BlockSpec((1, d), lambda i, j: (0, 0)), pl.BlockSpec((1, d), lambda i, j: (0, 0))],
        out_specs=pl.BlockSpec((tn, d), lambda i, j: (i, 0)),
        out_shape=jax.ShapeDtypeStruct((t, d), F32),
        scratch_shapes=[pltpu.VMEM((tn, d), BF16), pltpu.VMEM((tn, d), F32), pltpu.VMEM((te, tn), BF16)],
        compiler_params=_cp("arbitrary", "arbitrary"),
        name="peer",
    )(x, ub, vb, *stats, g.reshape(1, -1), beta.reshape(1, -1))


def _shortconv_kernel(cb_ref, cc_ref, ch_ref, ccp_ref, chp_ref, w_ref, o_ref, ext_ref, *, ts):
    i = pl.program_id(1)
    prev = ccp_ref[...] * chp_ref[...]
    ext_ref[0:SUBLANES, :] = jnp.where(i > 0, prev, 0.0)
    ext_ref[SUBLANES:, :] = cc_ref[...] * ch_ref[...]
    acc = jnp.zeros(o_ref.shape, F32)
    for k in range(CONV_W):
        off = SUBLANES - (CONV_W - 1) + k
        acc = acc + w_ref[k:k + 1, :] * ext_ref[off:off + ts, :]
    o_ref[...] = cb_ref[...] * acc


def _shortconv(y1, conv_w, ts=512):
    bsz, seq, _ = y1.shape
    blk = lambda c: pl.BlockSpec((None, ts, CONV_CH), lambda b, i, c=c: (b, i, c))
    halo = lambda c: pl.BlockSpec((None, SUBLANES, CONV_CH),
                                  lambda b, i, c=c: (b, jnp.maximum(i * (ts // SUBLANES) - 1, 0), c))
    kern = functools.partial(_shortconv_kernel, ts=ts)
    return pl.pallas_call(
        kern,
        grid=(bsz, seq // ts),
        in_specs=[blk(0), blk(1), blk(2), halo(1), halo(2),
                  pl.BlockSpec((CONV_W, CONV_CH), lambda b, i: (0, 0))],
        out_specs=pl.BlockSpec((None, ts, CONV_CH), lambda b, i: (b, i, 0)),
        out_shape=jax.ShapeDtypeStruct((bsz, seq, CONV_CH), F32),
        scratch_shapes=[pltpu.VMEM((ts + SUBLANES, CONV_CH), F32)],
        compiler_params=_cp("arbitrary", "arbitrary"),
        name="short_conv",
    )(y1, y1, y1, y1, y1, conv_w)


def _ssdconv_kernel(u_ref, up_ref, w_ref, b_ref, o_ref, ext_ref, *, ts):
    i = pl.program_id(1)
    ext_ref[0:SUBLANES, :] = jnp.where(i > 0, up_ref[...], 0.0)
    ext_ref[SUBLANES:, :] = u_ref[...]
    acc = jnp.zeros(o_ref.shape, F32) + b_ref[...]
    for k in range(SSD_CONV_W):
        off = SUBLANES - (SSD_CONV_W - 1) + k
        acc = acc + w_ref[k:k + 1, :] * ext_ref[off:off + ts, :]
    o_ref[...] = acc * jax.nn.sigmoid(acc)


def _ssdconv(y1, w, bias, ts=512, tc=1024):
    bsz, seq, _ = y1.shape
    nch = w.shape[1]
    c0 = L1_XS // tc
    kern = functools.partial(_ssdconv_kernel, ts=ts)
    return pl.pallas_call(
        kern,
        grid=(bsz, seq // ts, nch // tc),
        in_specs=[pl.BlockSpec((None, ts, tc), lambda b, i, c: (b, i, c0 + c)),
                  pl.BlockSpec((None, SUBLANES, tc),
                               lambda b, i, c: (b, jnp.maximum(i * (ts // SUBLANES) - 1, 0), c0 + c)),
                  pl.BlockSpec((SSD_CONV_W, tc), lambda b, i, c: (0, c)),
                  pl.BlockSpec((1, tc), lambda b, i, c: (0, c))],
        out_specs=pl.BlockSpec((None, ts, tc), lambda b, i, c: (b, i, c)),
        out_shape=jax.ShapeDtypeStruct((bsz, seq, nch), F32),
        scratch_shapes=[pltpu.VMEM((ts + SUBLANES, tc), F32)],
        compiler_params=_cp("arbitrary", "arbitrary", "arbitrary"),
        name="ssd_conv",
    )(y1, y1, w, bias.reshape(1, -1))


def _ssd_kernel(xs_ref, bs_ref, cs_ref, z_ref, dt_ref, dtb_ref, aneg_ref, dskip_ref, ng_ref, sel_ref, pk_ref,
                o_ref, state_ref):
    c = pl.program_id(1)
    nb = xs_ref.shape[0]
    lc = SSD_CHUNK
    npair = SSD_REP // 2
    bs_ = range(nb)
    prs = range(npair)
    inst = [(b, pr) for b in bs_ for pr in prs]

    @pl.when(c == 0)
    def _():
        state_ref[...] = jnp.zeros_like(state_ref)

    ri = lax.broadcasted_iota(jnp.int32, (lc, lc), 0)
    ci = lax.broadcasted_iota(jnp.int32, (lc, lc), 1)
    causal = ci <= ri
    tril = jnp.where(causal, 1.0, 0.0)
    lane = lax.broadcasted_iota(jnp.int32, (lc, LANES), 1)
    lo_half = lane < SSD_HEAD_DIM
    row_lo = lax.broadcasted_iota(jnp.int32, (LANES, 1), 0) < SSD_HEAD_DIM
    sl = lambda pr: slice(pr * LANES, (pr + 1) * LANES)
    spread = lambda m: _dot(m.astype(BF16), pk_ref[...])

    dt = [jax.nn.softplus(dt_ref[b] + dtb_ref[...]) for b in bs_]
    a_cs = [jnp.dot(tril, dt[b] * aneg_ref[...], preferred_element_type=F32, precision=lax.Precision.HIGHEST)
            for b in bs_]
    a_cs_t = [a.T for a in a_cs]
    a_last = [a[lc - 1:lc, :] for a in a_cs]
    decay_s = [spread(jnp.exp(a_last[b] - a_cs[b])) for b in bs_]
    decay_o = [spread(jnp.exp(a_cs[b])) for b in bs_]
    dt_p = [spread(dt[b]) for b in bs_]
    chunk_decay = [jnp.exp(a_last[b]) for b in bs_]
    bc = [bs_ref[b].astype(BF16) for b in bs_]
    cc = [cs_ref[b].astype(BF16) for b in bs_]
    cb = [_dot_nt(cc[b], bc[b]) for b in bs_]

    heads = [(b, r) for b in bs_ for r in range(SSD_REP)]
    hi = [a.astype(BF16) for a in a_cs]
    lo = [(a_cs[b] - hi[b].astype(F32)).astype(BF16) for b in bs_]
    col = [_dot(jnp.concatenate([hi[b], lo[b]], axis=1), sel_ref[...]) for b in bs_]
    seg = {k: col[k[0]][:, k[1] * LANES:(k[1] + 1) * LANES] - a_cs_t[k[0]][k[1]:k[1] + 1, :] for k in heads}
    dec = {k: jnp.exp(jnp.where(causal, seg[k], NEG)) for k in heads}
    ms = {k: (cb[k[0]] * dec[k]).astype(BF16) for k in heads}

    x = {k: xs_ref[k[0], :, sl(k[1])] for k in inst}
    xdt = {k: x[k] * dt_p[k[0]][:, sl(k[1])] for k in inst}
    x_lo = {k: jnp.where(lo_half, xdt[k], 0.0).astype(BF16) for k in inst}
    x_hi = {k: jnp.where(lo_half, 0.0, xdt[k]).astype(BF16) for k in inst}
    y = {k: _dot(ms[k[0], 2 * k[1]], x_lo[k]) + _dot(ms[k[0], 2 * k[1] + 1], x_hi[k]) for k in inst}
    prev = {k: state_ref[k[0], sl(k[1]), :] for k in inst}
    yoff = {k: _dot_nt(cc[k[0]], prev[k].astype(BF16)) * decay_o[k[0]][:, sl(k[1])] for k in inst}
    st = {k: _dot_tn((xdt[k] * decay_s[k[0]][:, sl(k[1])]).astype(BF16), bc[k[0]]) for k in inst}
    for k in inst:
        b, pr = k
        cd = jnp.where(row_lo, chunk_decay[b][:, 2 * pr:2 * pr + 1], chunk_decay[b][:, 2 * pr + 1:2 * pr + 2])
        state_ref[b, sl(pr), :] = prev[k] * cd + st[k]
    hz = {}
    for k in inst:
        b, pr = k
        z = z_ref[b, :, sl(pr)]
        hz[k] = (y[k] + yoff[k] + dskip_ref[:, sl(pr)] * x[k]) * (z * jax.nn.sigmoid(z))
    for b in bs_:
        ssq = sum(jnp.sum(hz[b, pr] * hz[b, pr], axis=-1, keepdims=True) for pr in prs)
        inv = lax.rsqrt(ssq / (SSD_REP * SSD_HEAD_DIM) + EPS)
        for pr in prs:
            o_ref[b, :, sl(pr)] = hz[b, pr] * inv * ng_ref[:, sl(pr)]


def _ssd(y1, xbc, dt_bias_g, a_neg_g, d_skip, norm_g):
    bsz, seq, _ = y1.shape
    gw = SSD_REP * SSD_HEAD_DIM
    nchunk = seq // SSD_CHUNK
    vec = pl.BlockSpec((None, 1, LANES), lambda g, c: (g, 0, 0))
    k = np.arange(2 * LANES)[:, None] % LANES
    lanes = np.arange(SSD_REP * LANES)[None, :]
    sel = (k == lanes // LANES).astype(np.float32)
    hd = np.arange(LANES)[:, None]
    lanes = np.arange(gw)[None, :]
    pk = (hd == lanes // SSD_HEAD_DIM).astype(np.float32)
    return pl.pallas_call(
        _ssd_kernel,
        grid=(SSD_GROUPS, nchunk),
        in_specs=[
            pl.BlockSpec((bsz, SSD_CHUNK, gw), lambda g, c: (0, c, g)),
            pl.BlockSpec((bsz, SSD_CHUNK, SSD_STATE), lambda g, c: (0, c, SSD_INNER // SSD_STATE + g)),
            pl.BlockSpec((bsz, SSD_CHUNK, SSD_STATE), lambda g, c: (0, c, SSD_INNER // SSD_STATE + SSD_GROUPS + g)),
            pl.BlockSpec((bsz, SSD_CHUNK, gw), lambda g, c: (0, c, L1_Z // gw + g)),
            pl.BlockSpec((bsz, SSD_CHUNK, LANES), lambda g, c: (0, c, L1_DT // LANES + g)),
            vec, vec,
            pl.BlockSpec((1, gw), lambda g, c: (0, g)),
            pl.BlockSpec((1, gw), lambda g, c: (0, g)),
            pl.BlockSpec((2 * LANES, SSD_REP * LANES), lambda g, c: (0, 0)),
            pl.BlockSpec((LANES, gw), lambda g, c: (0, 0)),
        ],
        out_specs=pl.BlockSpec((bsz, SSD_CHUNK, gw), lambda g, c: (0, c, g)),
        out_shape=jax.ShapeDtypeStruct((bsz, seq, SSD_INNER), F32),
        scratch_shapes=[pltpu.VMEM((bsz, gw, SSD_STATE), F32)],
        compiler_params=_cp("arbitrary", "arbitrary"),
        name="ssd",
    )(xbc, xbc, xbc, y1, y1, dt_bias_g, a_neg_g, jnp.repeat(d_skip, SSD_HEAD_DIM).reshape(1, -1),
      norm_g.reshape(1, -1), jnp.asarray(sel, dtype=BF16), jnp.asarray(pk, dtype=BF16))


def _pad_cols(w, n):
    return jnp.pad(w, ((0, 0), (0, n - w.shape[1])))


def _pack_w_in0(w):
    o = np.cumsum((0, A_HEADS * HEAD_DIM, A_KV_RANK, IDX_HEADS * IDX_DIM, IDX_DIM, IDX_HEADS,
                   B_HEADS * HEAD_DIM) + (B_KV_GROUPS * HEAD_DIM,) * 6 + (B_HEADS * 3,))
    seg = lambda k: w[:, int(o[k]):int(o[k + 1])]
    gates = seg(12)
    ng = gates.shape[1] // B_KV_GROUPS
    parts = [seg(0), seg(2), seg(5)] + [seg(k) for k in range(6, 12)] + [
        seg(1), _pad_cols(seg(3), LANES), _pad_cols(seg(4), LANES)] + [
        _pad_cols(gates[:, g * ng:(g + 1) * ng], LANES) for g in range(B_KV_GROUPS)]
    return jnp.concatenate(parts, axis=1).astype(BF16)


def _pack_w_in1(w):
    dt = w[:, L1_DT:]
    parts = [w[:, :L1_DT]] + [_pad_cols(dt[:, g * SSD_REP:(g + 1) * SSD_REP], LANES) for g in range(SSD_GROUPS)]
    return jnp.concatenate(parts, axis=1).astype(BF16)


def _per_group(v):
    return jnp.pad(v.reshape(SSD_GROUPS, 1, SSD_REP), ((0, 0), (0, 0), (0, LANES - SSD_REP)))


def _peer_block(x, wq, keys, u, v, g, b):
    stats = _peer_stats(x, wq, keys)
    return _peer_ln(x, stats, u, v, g, b)


def _even_layer(x, bsz, seq, w_in, kv_gain, w_uk, w_uv, cmpk_pos, cmpk_w1, cmpk_w2, cmpv_pos, cmpv_w1, cmpv_w2,
                w_out, ln1_g, ln1_b, peer_wq, peer_keys, peer_u, peer_v, ln2_g, ln2_b):
    t = bsz * seq
    y0 = _matmul(x, _pack_w_in0(w_in)).reshape(bsz, seq, NP0)
    o_a = _dsa(y0, kv_gain, w_uk, w_uv)
    kvc = _nsa_compress(y0, jnp.stack([cmpk_pos, cmpv_pos]), jnp.stack([cmpk_w1, cmpv_w1]),
                        jnp.stack([cmpk_w2, cmpv_w2]))
    o_b = _nsa(y0, kvc)
    x = _out_ln(o_a.reshape(t, -1), o_b.reshape(t, -1), w_out, x, ln1_g, ln1_b)
    return _peer_block(x, peer_wq, peer_keys, peer_u, peer_v, ln2_g, ln2_b)


def _odd_layer(x, bsz, seq, w_in, conv_w, ssd_conv_w, ssd_conv_b, dt_bias, a_log, d_skip, norm_g,
               w_out, ln1_g, ln1_b, peer_wq, peer_keys, peer_u, peer_v, ln2_g, ln2_b):
    t = bsz * seq
    y1 = _matmul(x, _pack_w_in1(w_in), tn=1536).reshape(bsz, seq, NP1)
    y_c = _shortconv(y1, conv_w)
    xbc = _ssdconv(y1, ssd_conv_w, ssd_conv_b)
    y_d = _ssd(y1, xbc, _per_group(dt_bias), _per_group(-jnp.exp(a_log)), d_skip, norm_g)
    x = _out_ln(y_c.reshape(t, -1), y_d.reshape(t, -1), w_out, x, ln1_g, ln1_b)
    return _peer_block(x, peer_wq, peer_keys, peer_u, peer_v, ln2_g, ln2_b)


def kernel(x, l0_w_in, l0_kv_gain, l0_w_uk, l0_w_uv, l0_cmpk_pos, l0_cmpk_w1, l0_cmpk_w2, l0_cmpv_pos, l0_cmpv_w1, l0_cmpv_w2, l0_w_out, l0_ln1_g, l0_ln1_b, l0_peer_wq, l0_peer_keys, l0_peer_u, l0_peer_v, l0_ln2_g, l0_ln2_b, l1_w_in, l1_conv_w, l1_ssd_conv_w, l1_ssd_conv_b, l1_dt_bias, l1_a_log, l1_d_skip, l1_norm_g, l1_w_out, l1_ln1_g, l1_ln1_b, l1_peer_wq, l1_peer_keys, l1_peer_u, l1_peer_v, l1_ln2_g, l1_ln2_b):
    bsz, seq, d = x.shape
    h = x.reshape(bsz * seq, d)
    h = _even_layer(h, bsz, seq, l0_w_in, l0_kv_gain, l0_w_uk, l0_w_uv, l0_cmpk_pos, l0_cmpk_w1, l0_cmpk_w2,
                    l0_cmpv_pos, l0_cmpv_w1, l0_cmpv_w2, l0_w_out, l0_ln1_g, l0_ln1_b,
                    l0_peer_wq, l0_peer_keys, l0_peer_u, l0_peer_v, l0_ln2_g, l0_ln2_b)
    h = _odd_layer(h, bsz, seq, l1_w_in, l1_conv_w, l1_ssd_conv_w, l1_ssd_conv_b, l1_dt_bias, l1_a_log,
                   l1_d_skip, l1_norm_g, l1_w_out, l1_ln1_g, l1_ln1_b,
                   l1_peer_wq, l1_peer_keys, l1_peer_u, l1_peer_v, l1_ln2_g, l1_ln2_b)
    return h.reshape(bsz, seq, d)
```

```python
import functools

import numpy as np
import jax
import jax.numpy as jnp
from jax import lax
from jax.experimental import pallas as pl
from jax.experimental.pallas import tpu as pltpu

F32 = jnp.float32
BF16 = jnp.bfloat16

D_MODEL = 2048
HEAD_DIM = 128
Q_BLOCK = 128
NEG = -1e30
FORCE = 1e30
EPS = 1e-5

A_HEADS = 8
A_KV_RANK = 256
IDX_HEADS = 16
IDX_DIM = 64
DSA_TOPK_MAX = 256
DSA_KEY_CHUNK = 512
DSA_SUM_ROWS = 16

B_HEADS = 8
B_KV_GROUPS = 2
NSA_CMP_LEN = 32
NSA_CMP_STRIDE = 16
NSA_SLC_LEN = 64
NSA_SLC_N = 16
NSA_WINDOW = 512
NSA_KEY_CHUNK = 512
NSA_SUM_ROWS = 16

CONV_CH = 1024
CONV_W = 3

SSD_HEAD_DIM = 64
SSD_HEADS = 48
SSD_INNER = SSD_HEADS * SSD_HEAD_DIM
SSD_GROUPS = 8
SSD_STATE = 128
SSD_CONV_W = 4
SSD_CHUNK = 128
SSD_REP = SSD_HEADS // SSD_GROUPS

PEER_HEADS = 8
PEER_NKEYS = 128
PEER_N = PEER_NKEYS * PEER_NKEYS
PEER_DKEY = 256
PEER_TOPK_HALF = 16
PEER_TOPK = 16

DEPTH = 2
ALPHA = (2 * DEPTH) ** 0.25

LANES = 128
SUBLANES = 8
VMEM_LIMIT = 56 * 1024 * 1024

L0_Q, L0_IQ, L0_BQ, L0_KV, L0_CKV, L0_IK, L0_IW, L0_GATE = 0, 1024, 2048, 3072, 4608, 4864, 4992, 5120
NP0 = 5376
L1_Z, L1_XS, L1_BS, L1_CS, L1_DT = 3072, 6144, 9216, 10240, 11264
NP1 = L1_DT + SSD_GROUPS * LANES


def _cp(*sem):
    return pltpu.CompilerParams(dimension_semantics=sem, vmem_limit_bytes=VMEM_LIMIT)


def _dot(a, b):
    return jnp.dot(a, b, preferred_element_type=F32)


def _dot_nt(a, b):
    return lax.dot_general(a, b, (((1,), (1,)), ((), ())), preferred_element_type=F32)


def _dot_tn(a, b):
    return lax.dot_general(a, b, (((0,), (0,)), ((), ())), preferred_element_type=F32)


def _layer_norm(v, g, b):
    mu = jnp.mean(v, axis=-1, keepdims=True)
    d = v - mu
    var = jnp.mean(d * d, axis=-1, keepdims=True)
    return d * lax.rsqrt(var + EPS) * g + b


def _mm_kernel(x_ref, w_ref, o_ref, xb_ref):
    @pl.when(pl.program_id(1) == 0)
    def _():
        xb_ref[...] = x_ref[...].astype(BF16)

    o_ref[...] = _dot(xb_ref[...], w_ref[...])


def _matmul(x, w, tm=1024, tn=768):
    t, k = x.shape
    n = w.shape[1]
    return pl.pallas_call(
        _mm_kernel,
        grid=(t // tm, n // tn),
        in_specs=[pl.BlockSpec((tm, k), lambda i, j: (i, 0)),
                  pl.BlockSpec((k, tn), lambda i, j: (0, j))],
        out_specs=pl.BlockSpec((tm, tn), lambda i, j: (i, j)),
        out_shape=jax.ShapeDtypeStruct((t, n), F32),
        scratch_shapes=[pltpu.VMEM((tm, k), BF16)],
        compiler_params=_cp("arbitrary", "arbitrary"),
        name="in_proj",
    )(x, w)


def _dsa_kernel(q_ref, iq_ref, iw_ref, ckv_ref, ik_ref, gain_ref, wuk_ref, wuv_ref, o_ref,
                ckvn_ref, ckvt_ref, iklo_ref, ikhi_ref, key_ref, bias_ref, qlt_ref, m_ref, acc_ref,
                *, seq, topk):
    i = pl.program_id(1)

    @pl.when(i == 0)
    def _():
        c = ckv_ref[...]
        ms = jnp.mean(c * c, axis=-1, keepdims=True)
        cn = c * lax.rsqrt(ms + EPS) * gain_ref[...]
        ckvn_ref[...] = cn.astype(BF16)
        ckvt_ref[0:A_KV_RANK, :] = cn.T.astype(BF16)
        ckvt_ref[A_KV_RANK:, :] = jnp.ones((DSA_SUM_ROWS, seq), BF16)
        ik = ik_ref[...]
        iklo_ref[...] = ik.astype(BF16)
        ikhi_ref[...] = pltpu.roll(ik, IDX_DIM, axis=1).astype(BF16)

    kc = DSA_KEY_CHUNK
    nch = (i * Q_BLOCK + Q_BLOCK + kc - 1) // kc
    qpos = i * Q_BLOCK + lax.broadcasted_iota(jnp.int32, (kc, Q_BLOCK), 1)
    kiota = lax.broadcasted_iota(jnp.int32, (kc, Q_BLOCK), 0)

    def chunk(c):
        return pl.ds(pl.multiple_of(c * kc, kc), kc)

    iwt = iw_ref[...].T * ((IDX_DIM * IDX_HEADS) ** -0.5)
    npair = IDX_HEADS // 2
    iq2 = [jnp.concatenate([iq_ref[:, p * LANES:(p + 1) * LANES], iq_ref[:, (p + 1) * LANES:(p + 2) * LANES]],
                           axis=0).astype(BF16) for p in range(0, npair, 2)]

    def index_chunk(c, carry):
        lo = iklo_ref[chunk(c), :]
        hi = ikhi_ref[chunk(c), :]
        score = jnp.zeros((kc, Q_BLOCK), F32)
        for t, iqt in enumerate(iq2):
            d_lo = jnp.maximum(_dot_nt(lo, iqt), 0.0)
            d_hi = jnp.maximum(_dot_nt(hi, iqt), 0.0)
            for u in range(2):
                h0 = 4 * t + 2 * u
                score = score + d_lo[:, u * Q_BLOCK:(u + 1) * Q_BLOCK] * iwt[h0:h0 + 1, :]
                score = score + d_hi[:, u * Q_BLOCK:(u + 1) * Q_BLOCK] * iwt[h0 + 1:h0 + 2, :]
        score = jnp.where(c * kc + kiota <= qpos, score, NEG)
        bits = lax.bitcast_convert_type(score, jnp.int32)
        key_ref[chunk(c), :] = jnp.where(bits < 0, bits ^ jnp.int32(0x7FFFFFFF), bits)
        return carry

    lax.fori_loop(0, nch, index_chunk, 0)

    def count_ge(cand):
        def body(c, acc):
            ge = jnp.where(key_ref[chunk(c), :] >= cand, 1.0, 0.0)
            while ge.shape[0] > SUBLANES:
                half = ge.shape[0] // 2
                ge = ge[:half] + ge[half:]
            return acc + ge
        acc = lax.fori_loop(0, nch, body, jnp.zeros((SUBLANES, Q_BLOCK), F32))
        return jnp.sum(acc, axis=0, keepdims=True)

    kf = float(topk)
    zero = jnp.zeros((1, Q_BLOCK), jnp.int32)
    prefix = jnp.where(count_ge(zero) >= kf, zero, jnp.int32(-2 ** 31))

    def bit_step(t, prefix):
        cand = prefix + lax.shift_left(jnp.int32(1), 30 - t)
        return jnp.where(count_ge(cand) >= kf, cand, prefix)

    prefix = lax.fori_loop(0, 31, bit_step, prefix)

    def bias_chunk(c, carry):
        keep = (key_ref[chunk(c), :] >= prefix) & (c * kc + kiota <= qpos)
        bias_ref[chunk(c), :] = jnp.where(keep, 0.0, NEG).astype(BF16)
        return carry

    lax.fori_loop(0, nch, bias_chunk, 0)

    scale = HEAD_DIM ** -0.5
    for h in range(A_HEADS):
        qh = q_ref[:, h * HEAD_DIM:(h + 1) * HEAD_DIM].astype(BF16)
        qlt_ref[:, h * Q_BLOCK:(h + 1) * Q_BLOCK] = (_dot_nt(wuk_ref[h], qh) * scale).astype(BF16)
    m_ref[...] = jnp.full(m_ref.shape, -jnp.inf, F32)
    acc_ref[...] = jnp.zeros_like(acc_ref)

    def attend(c, carry):
        s = _dot(ckvn_ref[chunk(c), :], qlt_ref[...]).astype(BF16)
        s = s + jnp.concatenate([bias_ref[chunk(c), :]] * A_HEADS, axis=1)
        m_old = m_ref[...]
        m_new = jnp.maximum(m_old, jnp.max(s, axis=0, keepdims=True).astype(F32))
        p = jnp.exp(s - m_new.astype(BF16))
        alpha = jnp.exp(m_old - m_new)
        acc_ref[...] = alpha * acc_ref[...] + _dot(ckvt_ref[:, chunk(c)], p)
        m_ref[...] = m_new
        return carry

    lax.fori_loop(0, nch, attend, 0)
    olt = (acc_ref[0:A_KV_RANK, :] / acc_ref[A_KV_RANK:A_KV_RANK + 1, :]).astype(BF16)
    for h in range(A_HEADS):
        o_ref[:, h * HEAD_DIM:(h + 1) * HEAD_DIM] = _dot_tn(olt[:, h * Q_BLOCK:(h + 1) * Q_BLOCK], wuv_ref[h])


def _dsa(y0, kv_gain, w_uk, w_uv):
    bsz, seq, _ = y0.shape
    topk = min(DSA_TOPK_MAX, seq // 4)
    nb = seq // Q_BLOCK
    nh = A_HEADS * Q_BLOCK
    kern = functools.partial(_dsa_kernel, seq=seq, topk=topk)
    return pl.pallas_call(
        kern,
        grid=(bsz, nb),
        in_specs=[
            pl.BlockSpec((None, Q_BLOCK, 1024), lambda b, i: (b, i, L0_Q // 1024)),
            pl.BlockSpec((None, Q_BLOCK, 1024), lambda b, i: (b, i, L0_IQ // 1024)),
            pl.BlockSpec((None, Q_BLOCK, LANES), lambda b, i: (b, i, L0_IW // LANES)),
            pl.BlockSpec((None, seq, A_KV_RANK), lambda b, i: (b, 0, L0_CKV // A_KV_RANK)),
            pl.BlockSpec((None, seq, LANES), lambda b, i: (b, 0, L0_IK // LANES)),
            pl.BlockSpec((1, A_KV_RANK), lambda b, i: (0, 0)),
            pl.BlockSpec((A_HEADS, A_KV_RANK, HEAD_DIM), lambda b, i: (0, 0, 0)),
            pl.BlockSpec((A_HEADS, A_KV_RANK, HEAD_DIM), lambda b, i: (0, 0, 0)),
        ],
        out_specs=pl.BlockSpec((None, Q_BLOCK, A_HEADS * HEAD_DIM), lambda b, i: (b, i, 0)),
        out_shape=jax.ShapeDtypeStruct((bsz, seq, A_HEADS * HEAD_DIM), F32),
        scratch_shapes=[pltpu.VMEM((seq, A_KV_RANK), BF16), pltpu.VMEM((A_KV_RANK + DSA_SUM_ROWS, seq), BF16),
                        pltpu.VMEM((seq, LANES), BF16), pltpu.VMEM((seq, LANES), BF16),
                        pltpu.VMEM((seq, Q_BLOCK), jnp.int32), pltpu.VMEM((seq, Q_BLOCK), BF16),
                        pltpu.VMEM((A_KV_RANK, nh), BF16),
                        pltpu.VMEM((1, nh), F32),
                        pltpu.VMEM((A_KV_RANK + DSA_SUM_ROWS, nh), F32)],
        compiler_params=_cp("arbitrary", "arbitrary"),
        name="dsa",
    )(y0, y0, y0, y0, y0, kv_gain.reshape(1, -1), w_uk.astype(BF16), w_uv.astype(BF16))


def _nsa_cmp_kernel(x_ref, pos_ref, w1_ref, w2_ref, o_ref, *, nblk):
    half = NSA_CMP_LEN // 2
    a = jnp.zeros((nblk, HEAD_DIM), F32)
    b = jnp.zeros((nblk, HEAD_DIM), F32)
    for l in range(half):
        rows = x_ref[pl.ds(l, nblk, stride=NSA_CMP_STRIDE), :]
        a = a + _dot((rows + pos_ref[l:l + 1, :]).astype(BF16), w1_ref[l].astype(BF16))
        b = b + _dot((rows + pos_ref[l + half:l + half + 1, :]).astype(BF16),
                     w1_ref[l + half].astype(BF16))
    h = jax.nn.gelu(a + pltpu.roll(b, nblk - 1, axis=0))
    o_ref[...] = _dot(h.astype(BF16), w2_ref[...].astype(BF16))


def _nsa_compress(y0, pos, w1, w2):
    bsz, seq, _ = y0.shape
    nblk = seq // NSA_CMP_STRIDE
    kern = functools.partial(_nsa_cmp_kernel, nblk=nblk)
    return pl.pallas_call(
        kern,
        grid=(bsz, 2, B_KV_GROUPS),
        in_specs=[
            pl.BlockSpec((None, seq, HEAD_DIM), lambda b, t, g: (b, 0, L0_KV // HEAD_DIM + 2 * t + g)),
            pl.BlockSpec((None, NSA_CMP_LEN, HEAD_DIM), lambda b, t, g: (t, 0, 0)),
            pl.BlockSpec((None, NSA_CMP_LEN, HEAD_DIM, HEAD_DIM), lambda b, t, g: (t, 0, 0, 0)),
            pl.BlockSpec((None, HEAD_DIM, HEAD_DIM), lambda b, t, g: (t, 0, 0)),
        ],
        out_specs=pl.BlockSpec((None, None, None, nblk, HEAD_DIM), lambda b, t, g: (b, t, g, 0, 0)),
        out_shape=jax.ShapeDtypeStruct((bsz, 2, B_KV_GROUPS, nblk, HEAD_DIM), F32),
        compiler_params=_cp("arbitrary", "arbitrary", "arbitrary"),
        name="nsa_compress",
    )(y0, pos, w1, w2)


def _nsa_kernel(q_ref, gate_ref, kc_ref, vc_ref, ks_ref, vs_ref, kw_ref, vw_ref, ovl_ref, exp_ref,
                o_ref, ksb_ref, vst_ref, kwb_ref, vwt_ref, kcb_ref, vct_ref, m_ref, acc_ref, *, seq, nsel):
    i = pl.program_id(2)
    hpg = B_HEADS // B_KV_GROUPS
    ncmp = kc_ref.shape[0]
    wlen = NSA_WINDOW + Q_BLOCK
    d = HEAD_DIM

    @pl.when(i == 0)
    def _():
        ones = lambda n: jnp.ones((NSA_SUM_ROWS, n), BF16)
        ksb_ref[...] = ks_ref[...].astype(BF16)
        kwb_ref[...] = kw_ref[...].astype(BF16)
        kcb_ref[...] = kc_ref[...].astype(BF16)
        vst_ref[0:d, :] = vs_ref[...].T.astype(BF16)
        vst_ref[d:, :] = ones(seq)
        vwt_ref[0:d, :] = vw_ref[...].T.astype(BF16)
        vwt_ref[d:, :] = ones(seq)
        vct_ref[0:d, :] = vc_ref[...].T.astype(BF16)
        vct_ref[d:, :] = ones(ncmp)

    start = i * Q_BLOCK
    scale = HEAD_DIM ** -0.5
    qt = jnp.concatenate([(q_ref[:, h * d:(h + 1) * d] * scale).T for h in range(hpg)], axis=1).astype(BF16)
    qlane = start + lax.broadcasted_iota(jnp.int32, (1, Q_BLOCK), 1)
    qlane4 = jnp.concatenate([qlane] * hpg, axis=1)

    def softmax_pv(s, vt):
        m = jnp.max(s, axis=0, keepdims=True)
        ext = _dot(vt, jnp.exp(s - m))
        return ext[0:d, :] / ext[d:d + 1, :]

    s_c = _dot(kcb_ref[...], qt)
    cend = lax.broadcasted_iota(jnp.int32, (ncmp, 1), 0) * NSA_CMP_STRIDE + (NSA_CMP_LEN - 1)
    s_c = jnp.where(cend <= qlane4, s_c, NEG)
    p_c = jnp.exp(s_c - jnp.max(s_c, axis=0, keepdims=True))
    p_c = p_c / jnp.sum(p_c, axis=0, keepdims=True)
    p_c = jnp.where(qlane4 >= NSA_CMP_LEN - 1, p_c, 0.0)
    o_c = _dot(vct_ref[0:d, :], p_c.astype(BF16))

    psum = p_c[:, 0:Q_BLOCK]
    for h in range(1, hpg):
        psum = psum + p_c[:, h * Q_BLOCK:(h + 1) * Q_BLOCK]
    imp = jnp.dot(ovl_ref[...], psum, preferred_element_type=F32, precision=lax.Precision.HIGHEST)
    nslc = seq // NSA_SLC_LEN
    jj = lax.broadcasted_iota(jnp.int32, (LANES, 1), 0)
    cur = qlane // NSA_SLC_LEN
    forced = (jj == 0) | (jj == cur) | (jj == cur - 1)
    imp = jnp.where(forced, FORCE, imp)
    imp = jnp.where(jj <= cur, imp, NEG)
    imp = jnp.where(jj < nslc, imp, -3e38)
    rank = jnp.zeros((LANES, Q_BLOCK), F32)
    for t in range(nslc):
        row = imp[t:t + 1, :]
        later = jnp.where(jj > t, 1.0, 0.0)
        rank = rank + jnp.where(row > imp, 1.0, jnp.where(row == imp, later, 0.0))
    selb = jnp.where(rank < float(nsel), 1.0, 0.0).astype(BF16)

    kc = NSA_KEY_CHUNK
    nch = (start + Q_BLOCK + kc - 1) // kc
    kiota = lax.broadcasted_iota(jnp.int32, (kc, 1), 0)
    m_ref[...] = jnp.full(m_ref.shape, -jnp.inf, F32)
    acc_ref[...] = jnp.zeros_like(acc_ref)

    def attend(c, carry):
        ck = pl.ds(pl.multiple_of(c * kc, kc), kc)
        keysel = _dot(exp_ref[ck, :], selb)
        keep = (keysel > 0.5) & (c * kc + kiota <= qlane)
        bias = jnp.where(keep, 0.0, NEG).astype(BF16)
        s = _dot(ksb_ref[ck, :], qt).astype(BF16) + jnp.concatenate([bias] * hpg, axis=1)
        m_old = m_ref[...]
        m_new = jnp.maximum(m_old, jnp.max(s, axis=0, keepdims=True).astype(F32))
        p = jnp.exp(s - m_new.astype(BF16))
        acc_ref[...] = jnp.exp(m_old - m_new) * acc_ref[...] + _dot(vst_ref[:, ck], p)
        m_ref[...] = m_new
        return carry

    lax.fori_loop(0, nch, attend, 0)
    o_s = acc_ref[0:d, :] / acc_ref[d:d + 1, :]

    ws = pl.multiple_of(jnp.maximum(start - NSA_WINDOW, 0), Q_BLOCK)
    kposw = ws + lax.broadcasted_iota(jnp.int32, (wlen, 1), 0)
    vis_w = (kposw <= qlane4) & (kposw > qlane4 - NSA_WINDOW)
    s_w = _dot(kwb_ref[pl.ds(ws, wlen), :], qt).astype(BF16)
    o_w = softmax_pv(jnp.where(vis_w, s_w, jnp.asarray(NEG, BF16)), vwt_ref[:, pl.ds(ws, wlen)])

    gate = jax.nn.sigmoid(gate_ref[...]).T
    for h in range(hpg):
        cs = slice(h * Q_BLOCK, (h + 1) * Q_BLOCK)
        o = (gate[3 * h:3 * h + 1, :] * o_c[:, cs] + gate[3 * h + 1:3 * h + 2, :] * o_s[:, cs]
             + gate[3 * h + 2:3 * h + 3, :] * o_w[:, cs])
        o_ref[:, h * d:(h + 1) * d] = o.T


def _nsa(y0, kvc):
    bsz, seq, _ = y0.shape
    nb = seq // Q_BLOCK
    nslc = seq // NSA_SLC_LEN
    nsel = min(NSA_SLC_N, nslc)
    ncmp = seq // NSA_CMP_STRIDE
    hpg = B_HEADS // B_KV_GROUPS
    cs = np.arange(ncmp)[None, :] * NSA_CMP_STRIDE
    ss = np.arange(LANES)[:, None] * NSA_SLC_LEN
    ovl = ((cs < ss + NSA_SLC_LEN) & (cs + NSA_CMP_LEN > ss) & (np.arange(ncmp)[None, :] < ncmp - 1)
           & (np.arange(LANES)[:, None] < nslc)).astype(np.float32)
    expand = (np.arange(seq)[:, None] // NSA_SLC_LEN == np.arange(LANES)[None, :]).astype(np.float32)
    kvb = L0_KV // HEAD_DIM
    full = lambda off: pl.BlockSpec((None, seq, HEAD_DIM), lambda b, g, i, off=off: (b, 0, kvb + off + g))
    kern = functools.partial(_nsa_kernel, seq=seq, nsel=nsel)
    dx = HEAD_DIM + NSA_SUM_ROWS
    return pl.pallas_call(
        kern,
        grid=(bsz, B_KV_GROUPS, nb),
        in_specs=[
            pl.BlockSpec((None, Q_BLOCK, hpg * HEAD_DIM), lambda b, g, i: (b, i, L0_BQ // (hpg * HEAD_DIM) + g)),
            pl.BlockSpec((None, Q_BLOCK, LANES), lambda b, g, i: (b, i, L0_GATE // LANES + g)),
            pl.BlockSpec((None, None, None, ncmp, HEAD_DIM), lambda b, g, i: (b, 0, g, 0, 0)),
            pl.BlockSpec((None, None, None, ncmp, HEAD_DIM), lambda b, g, i: (b, 1, g, 0, 0)),
            full(4), full(6), full(8), full(10),
            pl.BlockSpec((LANES, ncmp), lambda b, g, i: (0, 0)),
            pl.BlockSpec((seq, LANES), lambda b, g, i: (0, 0)),
        ],
        out_specs=pl.BlockSpec((None, Q_BLOCK, hpg * HEAD_DIM), lambda b, g, i: (b, i, g)),
        out_shape=jax.ShapeDtypeStruct((bsz, seq, B_HEADS * HEAD_DIM), F32),
        scratch_shapes=[pltpu.VMEM((seq, HEAD_DIM), BF16), pltpu.VMEM((dx, seq), BF16),
                        pltpu.VMEM((seq, HEAD_DIM), BF16), pltpu.VMEM((dx, seq), BF16),
                        pltpu.VMEM((ncmp, HEAD_DIM), BF16), pltpu.VMEM((dx, ncmp), BF16),
                        pltpu.VMEM((1, hpg * Q_BLOCK), F32), pltpu.VMEM((dx, hpg * Q_BLOCK), F32)],
        compiler_params=_cp("arbitrary", "arbitrary", "arbitrary"),
        name="nsa",
    )(y0, y0, kvc, kvc, y0, y0, y0, y0, jnp.asarray(ovl), jnp.asarray(expand, dtype=BF16))


def _out_ln_kernel(a_ref, b_ref, wa_ref, wb_ref, x_ref, g_ref, beta_ref, o_ref):
    acc = _dot(a_ref[...].astype(BF16), wa_ref[...]) + _dot(b_ref[...].astype(BF16), wb_ref[...])
    o_ref[...] = _layer_norm(ALPHA * x_ref[...] + acc, g_ref[...], beta_ref[...])


def _out_ln(a, b, w_out, x, g, beta, tm=256):
    t, ka = a.shape
    kb = b.shape[1]
    d = x.shape[1]
    wa = w_out[:ka].astype(BF16)
    wb = w_out[ka:].astype(BF16)
    return pl.pallas_call(
        _out_ln_kernel,
        grid=(t // tm,),
        in_specs=[pl.BlockSpec((tm, ka), lambda i: (i, 0)), pl.BlockSpec((tm, kb), lambda i: (i, 0)),
                  pl.BlockSpec((ka, d), lambda i: (0, 0)), pl.BlockSpec((kb, d), lambda i: (0, 0)),
                  pl.BlockSpec((tm, d), lambda i: (i, 0)),
                  pl.BlockSpec((1, d), lambda i: (0, 0)), pl.BlockSpec((1, d), lambda i: (0, 0))],
        out_specs=pl.BlockSpec((tm, d), lambda i: (i, 0)),
        out_shape=jax.ShapeDtypeStruct((t, d), F32),
        compiler_params=_cp("arbitrary"),
        name="out_proj_ln",
    )(a, b, wa, wb, x, g.reshape(1, -1), beta.reshape(1, -1))


BIG = 1e30


def _pack_pair(v):
    b = lax.bitcast_convert_type(v.astype(BF16).astype(F32), jnp.int32)
    return b | lax.shift_right_logical(b, 16)


def _cmp_exchange(rows, i, l):
    a, b = rows[i], rows[l]
    rows[i], rows[l] = jnp.maximum(a, b), jnp.minimum(a, b)


def _bitonic_merge_desc(rows):
    n = len(rows)
    j = n // 2
    while j >= 1:
        for i in range(n):
            if i ^ j > i:
                _cmp_exchange(rows, i, i ^ j)
        j //= 2
    return rows


def _bitonic_sort_desc(rows):
    n = len(rows)
    k = 2
    while k <= n:
        j = k // 2
        while j >= 1:
            for i in range(n):
                l = i ^ j
                if l > i:
                    if (i & k) == 0:
                        _cmp_exchange(rows, i, l)
                    else:
                        _cmp_exchange(rows, l, i)
            j //= 2
        k *= 2
    return rows


def _top_sorted(rows):
    n = len(rows)
    rows = _bitonic_sort_desc(list(rows))
    shift = SUBLANES // 2
    while shift >= 1:
        other = [pltpu.roll(r, shift, axis=0) for r in rows]
        rows = _bitonic_merge_desc([jnp.maximum(rows[k], other[n - 1 - k]) for k in range(n)])
        shift //= 2
    return rows


def _sum_sublanes(x):
    shift = SUBLANES // 2
    while shift >= 1:
        x = x + pltpu.roll(x, shift, axis=0)
        shift //= 2
    return x


def _max_sublanes(x):
    shift = SUBLANES // 2
    while shift >= 1:
        x = jnp.maximum(x, pltpu.roll(x, shift, axis=0))
        shift //= 2
    return x


def _count_sorted(vals, x, pred):
    n = len(vals)
    step = n // 2
    base = jnp.zeros_like(x)
    hist = []
    while step >= 1:
        options = [vals[b + step - 1] for b in range(0, n, 2 * step)]

        def choose(level, lo_idx):
            if level == len(hist):
                return options[lo_idx]
            span = len(options) >> (level + 1)
            return jnp.where(hist[level], choose(level + 1, lo_idx + span), choose(level + 1, lo_idx))

        p = pred(choose(0, 0), x)
        base = jnp.where(p, base + float(step), base)
        hist.append(p)
        step //= 2
    return jnp.where(pred(vals[n - 1], x), float(n), base)


def _peer_stats_kernel(x_ref, wqt_ref, keys_ref, r1_ref, c1_ref, rk2_ref, e2_ref, qt_ref):
    qt_ref[...] = _dot_nt(wqt_ref[...], x_ref[...].astype(BF16))
    tn = x_ref.shape[0]
    half = PEER_DKEY // 2
    kh = PEER_TOPK_HALF
    nrow = PEER_NKEYS // SUBLANES
    sub = lax.broadcasted_iota(jnp.int32, (SUBLANES, 1), 0)
    tall = lambda v: jnp.concatenate([v] * nrow, axis=0)
    split = lambda s: [s[r * SUBLANES:(r + 1) * SUBLANES, :] for r in range(nrow)]

    def spread(vals):
        out = vals[0]
        for s in range(1, SUBLANES):
            out = jnp.where(sub == s, vals[s], out)
        return out

    for h in range(PEER_HEADS):
        r0 = h * PEER_DKEY
        s1 = _dot(keys_ref[0], qt_ref[r0:r0 + half, :].astype(BF16))
        s2 = _dot(keys_ref[1], qt_ref[r0 + half:r0 + PEER_DKEY, :].astype(BF16))
        v1 = _top_sorted(split(s1))
        v2 = _top_sorted(split(s2))
        b_lo, b_hi, a_hi = spread(v2[:SUBLANES]), spread(v2[SUBLANES:]), spread(v1[SUBLANES:])
        cand = [v1[0] + b_lo, v1[0] + b_hi, v1[1] + b_lo]
        for a in range(2, SUBLANES):
            cand.append(jnp.where(sub < (PEER_TOPK + 1) // (a + 1), v1[a] + b_lo, -BIG))
        cand.append(a_hi + v2[0])
        pad = jnp.full((SUBLANES, tn), -BIG, F32)
        cv = _top_sorted(cand + [pad] * (PEER_TOPK - len(cand)))
        top, kth = cv[0], cv[PEER_TOPK - 1]
        cnt = jnp.zeros((SUBLANES, tn), F32)
        z = jnp.zeros((SUBLANES, tn), F32)
        below = jnp.full((SUBLANES, tn), -BIG, F32)
        for c in cand:
            ge = c >= kth
            cnt = cnt + jnp.where(ge, 1.0, 0.0)
            z = z + jnp.where(ge, jnp.exp(c - top), 0.0)
            below = jnp.maximum(below, jnp.where(ge, -BIG, c))
        z = _sum_sublanes(z)
        nxt = jnp.where(_sum_sublanes(cnt) > float(PEER_TOPK), kth, _max_sublanes(below))
        thr = 0.5 * (kth + nxt)
        act1 = s1 >= tall(v1[kh - 1])
        act2 = s2 >= tall(v2[kh - 1])
        need = tall(thr) - s1
        tv2 = [tall(v) for v in v2]
        r1 = _count_sorted(tv2, need, lambda v, x: v >= x)
        rk2 = _count_sorted(tv2, s2, lambda v, x: v > x)
        r1_ref[h] = _pack_pair(jnp.where(act1, r1, 0.0))
        c1_ref[h] = _pack_pair(jnp.where(act1, jnp.exp(s1 - tall(v1[0])) / tall(z), 0.0))
        rk2_ref[h] = rk2.astype(BF16)
        e2_ref[h] = jnp.where(act2, jnp.exp(s2 - tall(v2[0])), 0.0).astype(BF16)


def _peer_stats(x, w_q, sub_keys, tn=512):
    t, d = x.shape
    wqt = w_q.T.astype(BF16)
    shp = lambda dt: jax.ShapeDtypeStruct((PEER_HEADS, PEER_NKEYS, t), dt)
    spec = pl.BlockSpec((PEER_HEADS, PEER_NKEYS, tn), lambda i: (0, 0, i))
    return pl.pallas_call(
        _peer_stats_kernel,
        grid=(t // tn,),
        in_specs=[pl.BlockSpec((tn, d), lambda i: (i, 0)),
                  pl.BlockSpec((PEER_HEADS * PEER_DKEY, d), lambda i: (0, 0)),
                  pl.BlockSpec((2, PEER_NKEYS, PEER_DKEY // 2), lambda i: (0, 0, 0))],
        out_specs=[spec] * 4,
        out_shape=[shp(jnp.int32), shp(jnp.int32), shp(BF16), shp(BF16)],
        scratch_shapes=[pltpu.VMEM((PEER_HEADS * PEER_DKEY, tn), F32)],
        compiler_params=_cp("arbitrary"),
        name="peer_stats",
    )(x, wqt, sub_keys.astype(BF16))


def _peer_kernel(x_ref, u_ref, v_ref, r1_ref, c1_ref, rk2_ref, e2_ref, g_ref, beta_ref, o_ref,
                 xb_ref, acc_ref, w_ref, *, te):
    j = pl.program_id(1)
    tn = xb_ref.shape[0]
    pack = 2 * SUBLANES
    nsub = te // PEER_NKEYS

    @pl.when(j == 0)
    def _():
        xb_ref[...] = x_ref[...].astype(BF16)
        acc_ref[...] = jnp.zeros_like(acc_ref)

    def row_bf16(ref, h, idx):
        w = jnp.broadcast_to(ref[h, pl.ds(idx, 1), :], (SUBLANES, tn))
        return jnp.concatenate([pltpu.bitcast(w, BF16)] * (PEER_NKEYS // pack), axis=0)

    act = jax.nn.gelu(_dot_nt(u_ref[...], xb_ref[...]).astype(BF16))
    for ii in range(nsub):
        idx = j * nsub + ii
        gate = jnp.zeros((PEER_NKEYS, tn), BF16)
        for h in range(PEER_HEADS):
            sel = rk2_ref[h] < row_bf16(r1_ref, h, idx)
            gate = gate + jnp.where(sel, e2_ref[h], jnp.zeros((), BF16)) * row_bf16(c1_ref, h, idx)
        sl = slice(ii * PEER_NKEYS, (ii + 1) * PEER_NKEYS)
        w_ref[sl, :] = act[sl, :] * gate
    acc_ref[...] += _dot_tn(w_ref[...], v_ref[...])

    @pl.when(j == pl.num_programs(1) - 1)
    def _():
        o_ref[...] = _layer_norm(ALPHA * x_ref[...] + acc_ref[...], g_ref[...], beta_ref[...])


def _peer_ln(x, stats, u_tab, v_tab, g, beta, tn=512, te=1024):
    t, d = x.shape
    ne = u_tab.shape[0]
    ub = u_tab.astype(BF16)
    vb = v_tab.astype(BF16)
    sspec = pl.BlockSpec((PEER_HEADS, PEER_NKEYS, tn), lambda i, j: (0, 0, i), pipeline_mode=pl.Buffered(1))
    kern = functools.partial(_peer_kernel, te=te)
    return pl.pallas_call(
        kern,
        grid=(t // tn, ne // te),
        in_specs=[pl.BlockSpec((tn, d), lambda i, j: (i, 0)),
                  pl.BlockSpec((te, d), lambda i, j: (j, 0)),
                  pl.BlockSpec((te, d), lambda i, j: (j, 0)),
                  sspec, sspec, sspec, sspec,
                  pl.BlockSpec((1, d), lambda i, j: (0, 0)), pl.BlockSpec((1, d), lambda i, j: (0, 0))],
        out_specs=pl.BlockSpec((tn, d), lambda i, j: (i, 0)),
        out_shape=jax.ShapeDtypeStruct((t, d), F32),
        scratch_shapes=[pltpu.VMEM((tn, d), BF16), pltpu.VMEM((tn, d), F32), pltpu.VMEM((te, tn), BF16)],
        compiler_params=_cp("arbitrary", "arbitrary"),
        name="peer",
    )(x, ub, vb, *stats, g.reshape(1, -1), beta.reshape(1, -1))


def _shortconv_kernel(cb_ref, cc_ref, ch_ref, ccp_ref, chp_ref, w_ref, o_ref, ext_ref, *, ts):
    i = pl.program_id(1)
    prev = ccp_ref[...] * chp_ref[...]
    ext_ref[0:SUBLANES, :] = jnp.where(i > 0, prev, 0.0)
    ext_ref[SUBLANES:, :] = cc_ref[...] * ch_ref[...]
    acc = jnp.zeros(o_ref.shape, F32)
    for k in range(CONV_W):
        off = SUBLANES - (CONV_W - 1) + k
        acc = acc + w_ref[k:k + 1, :] * ext_ref[off:off + ts, :]
    o_ref[...] = cb_ref[...] * acc


def _shortconv(y1, conv_w, ts=512):
    bsz, seq, _ = y1.shape
    blk = lambda c: pl.BlockSpec((None, ts, CONV_CH), lambda b, i, c=c: (b, i, c))
    halo = lambda c: pl.BlockSpec((None, SUBLANES, CONV_CH),
                                  lambda b, i, c=c: (b, jnp.maximum(i * (ts // SUBLANES) - 1, 0), c))
    kern = functools.partial(_shortconv_kernel, ts=ts)
    return pl.pallas_call(
        kern,
        grid=(bsz, seq // ts),
        in_specs=[blk(0), blk(1), blk(2), halo(1), halo(2),
                  pl.BlockSpec((CONV_W, CONV_CH), lambda b, i: (0, 0))],
        out_specs=pl.BlockSpec((None, ts, CONV_CH), lambda b, i: (b, i, 0)),
        out_shape=jax.ShapeDtypeStruct((bsz, seq, CONV_CH), F32),
        scratch_shapes=[pltpu.VMEM((ts + SUBLANES, CONV_CH), F32)],
        compiler_params=_cp("arbitrary", "arbitrary"),
        name="short_conv",
    )(y1, y1, y1, y1, y1, conv_w)


def _ssdconv_kernel(u_ref, up_ref, w_ref, b_ref, o_ref, ext_ref, *, ts):
    i = pl.program_id(1)
    ext_ref[0:SUBLANES, :] = jnp.where(i > 0, up_ref[...], 0.0)
    ext_ref[SUBLANES:, :] = u_ref[...]
    acc = jnp.zeros(o_ref.shape, F32) + b_ref[...]
    for k in range(SSD_CONV_W):
        off = SUBLANES - (SSD_CONV_W - 1) + k
        acc = acc + w_ref[k:k + 1, :] * ext_ref[off:off + ts, :]
    o_ref[...] = acc * jax.nn.sigmoid(acc)


def _ssdconv(y1, w, bias, ts=512, tc=1024):
    bsz, seq, _ = y1.shape
    nch = w.shape[1]
    c0 = L1_XS // tc
    kern = functools.partial(_ssdconv_kernel, ts=ts)
    return pl.pallas_call(
        kern,
        grid=(bsz, seq // ts, nch // tc),
        in_specs=[pl.BlockSpec((None, ts, tc), lambda b, i, c: (b, i, c0 + c)),
                  pl.BlockSpec((None, SUBLANES, tc),
                               lambda b, i, c: (b, jnp.maximum(i * (ts // SUBLANES) - 1, 0), c0 + c)),
                  pl.BlockSpec((SSD_CONV_W, tc), lambda b, i, c: (0, c)),
                  pl.BlockSpec((1, tc), lambda b, i, c: (0, c))],
        out_specs=pl.BlockSpec((None, ts, tc), lambda b, i, c: (b, i, c)),
        out_shape=jax.ShapeDtypeStruct((bsz, seq, nch), F32),
        scratch_shapes=[pltpu.VMEM((ts + SUBLANES, tc), F32)],
        compiler_params=_cp("arbitrary", "arbitrary", "arbitrary"),
        name="ssd_conv",
    )(y1, y1, w, bias.reshape(1, -1))


def _ssd_kernel(xs_ref, bs_ref, cs_ref, z_ref, dt_ref, dtb_ref, aneg_ref, dskip_ref, ng_ref, sel_ref, pk_ref,
                o_ref, state_ref):
    c = pl.program_id(1)
    nb = xs_ref.shape[0]
    lc = SSD_CHUNK
    npair = SSD_REP // 2
    bs_ = range(nb)
    prs = range(npair)
    inst = [(b, pr) for b in bs_ for pr in prs]

    @pl.when(c == 0)
    def _():
        state_ref[...] = jnp.zeros_like(state_ref)

    ri = lax.broadcasted_iota(jnp.int32, (lc, lc), 0)
    ci = lax.broadcasted_iota(jnp.int32, (lc, lc), 1)
    causal = ci <= ri
    tril = jnp.where(causal, 1.0, 0.0)
    lane = lax.broadcasted_iota(jnp.int32, (lc, LANES), 1)
    lo_half = lane < SSD_HEAD_DIM
    row_lo = lax.broadcasted_iota(jnp.int32, (LANES, 1), 0) < SSD_HEAD_DIM
    sl = lambda pr: slice(pr * LANES, (pr + 1) * LANES)
    spread = lambda m: _dot(m.astype(BF16), pk_ref[...])

    dt = [jax.nn.softplus(dt_ref[b] + dtb_ref[...]) for b in bs_]
    a_cs = [jnp.dot(tril, dt[b] * aneg_ref[...], preferred_element_type=F32, precision=lax.Precision.HIGHEST)
            for b in bs_]
    a_cs_t = [a.T for a in a_cs]
    a_last = [a[lc - 1:lc, :] for a in a_cs]
    decay_s = [spread(jnp.exp(a_last[b] - a_cs[b])) for b in bs_]
    decay_o = [spread(jnp.exp(a_cs[b])) for b in bs_]
    dt_p = [spread(dt[b]) for b in bs_]
    chunk_decay = [jnp.exp(a_last[b]) for b in bs_]
    bc = [bs_ref[b].astype(BF16) for b in bs_]
    cc = [cs_ref[b].astype(BF16) for b in bs_]
    cb = [_dot_nt(cc[b], bc[b]) for b in bs_]

    heads = [(b, r) for b in bs_ for r in range(SSD_REP)]
    hi = [a.astype(BF16) for a in a_cs]
    lo = [(a_cs[b] - hi[b].astype(F32)).astype(BF16) for b in bs_]
    col = [_dot(jnp.concatenate([hi[b], lo[b]], axis=1), sel_ref[...]) for b in bs_]
    seg = {k: col[k[0]][:, k[1] * LANES:(k[1] + 1) * LANES] - a_cs_t[k[0]][k[1]:k[1] + 1, :] for k in heads}
    dec = {k: jnp.exp(jnp.where(causal, seg[k], NEG)) for k in heads}
    ms = {k: (cb[k[0]] * dec[k]).astype(BF16) for k in heads}

    x = {k: xs_ref[k[0], :, sl(k[1])] for k in inst}
    xdt = {k: x[k] * dt_p[k[0]][:, sl(k[1])] for k in inst}
    x_lo = {k: jnp.where(lo_half, xdt[k], 0.0).astype(BF16) for k in inst}
    x_hi = {k: jnp.where(lo_half, 0.0, xdt[k]).astype(BF16) for k in inst}
    y = {k: _dot(ms[k[0], 2 * k[1]], x_lo[k]) + _dot(ms[k[0], 2 * k[1] + 1], x_hi[k]) for k in inst}
    prev = {k: state_ref[k[0], sl(k[1]), :] for k in inst}
    yoff = {k: _dot_nt(cc[k[0]], prev[k].astype(BF16)) * decay_o[k[0]][:, sl(k[1])] for k in inst}
    st = {k: _dot_tn((xdt[k] * decay_s[k[0]][:, sl(k[1])]).astype(BF16), bc[k[0]]) for k in inst}
    for k in inst:
        b, pr = k
        cd = jnp.where(row_lo, chunk_decay[b][:, 2 * pr:2 * pr + 1], chunk_decay[b][:, 2 * pr + 1:2 * pr + 2])
        state_ref[b, sl(pr), :] = prev[k] * cd + st[k]
    hz = {}
    for k in inst:
        b, pr = k
        z = z_ref[b, :, sl(pr)]
        hz[k] = (y[k] + yoff[k] + dskip_ref[:, sl(pr)] * x[k]) * (z * jax.nn.sigmoid(z))
    for b in bs_:
        ssq = sum(jnp.sum(hz[b, pr] * hz[b, pr], axis=-1, keepdims=True) for pr in prs)
        inv = lax.rsqrt(ssq / (SSD_REP * SSD_HEAD_DIM) + EPS)
        for pr in prs:
            o_ref[b, :, sl(pr)] = hz[b, pr] * inv * ng_ref[:, sl(pr)]


def _ssd(y1, xbc, dt_bias_g, a_neg_g, d_skip, norm_g):
    bsz, seq, _ = y1.shape
    gw = SSD_REP * SSD_HEAD_DIM
    nchunk = seq // SSD_CHUNK
    vec = pl.BlockSpec((None, 1, LANES), lambda g, c: (g, 0, 0))
    k = np.arange(2 * LANES)[:, None] % LANES
    lanes = np.arange(SSD_REP * LANES)[None, :]
    sel = (k == lanes // LANES).astype(np.float32)
    hd = np.arange(LANES)[:, None]
    lanes = np.arange(gw)[None, :]
    pk = (hd == lanes // SSD_HEAD_DIM).astype(np.float32)
    return pl.pallas_call(
        _ssd_kernel,
        grid=(SSD_GROUPS, nchunk),
        in_specs=[
            pl.BlockSpec((bsz, SSD_CHUNK, gw), lambda g, c: (0, c, g)),
            pl.BlockSpec((bsz, SSD_CHUNK, SSD_STATE), lambda g, c: (0, c, SSD_INNER // SSD_STATE + g)),
            pl.BlockSpec((bsz, SSD_CHUNK, SSD_STATE), lambda g, c: (0, c, SSD_INNER // SSD_STATE + SSD_GROUPS + g)),
            pl.BlockSpec((bsz, SSD_CHUNK, gw), lambda g, c: (0, c, L1_Z // gw + g)),
            pl.BlockSpec((bsz, SSD_CHUNK, LANES), lambda g, c: (0, c, L1_DT // LANES + g)),
            vec, vec,
            pl.BlockSpec((1, gw), lambda g, c: (0, g)),
            pl.BlockSpec((1, gw), lambda g, c: (0, g)),
            pl.BlockSpec((2 * LANES, SSD_REP * LANES), lambda g, c: (0, 0)),
            pl.BlockSpec((LANES, gw), lambda g, c: (0, 0)),
        ],
        out_specs=pl.BlockSpec((bsz, SSD_CHUNK, gw), lambda g, c: (0, c, g)),
        out_shape=jax.ShapeDtypeStruct((bsz, seq, SSD_INNER), F32),
        scratch_shapes=[pltpu.VMEM((bsz, gw, SSD_STATE), F32)],
        compiler_params=_cp("arbitrary", "arbitrary"),
        name="ssd",
    )(xbc, xbc, xbc, y1, y1, dt_bias_g, a_neg_g, jnp.repeat(d_skip, SSD_HEAD_DIM).reshape(1, -1),
      norm_g.reshape(1, -1), jnp.asarray(sel, dtype=BF16), jnp.asarray(pk, dtype=BF16))


def _pad_cols(w, n):
    return jnp.pad(w, ((0, 0), (0, n - w.shape[1])))


def _pack_w_in0(w):
    o = np.cumsum((0, A_HEADS * HEAD_DIM, A_KV_RANK, IDX_HEADS * IDX_DIM, IDX_DIM, IDX_HEADS,
                   B_HEADS * HEAD_DIM) + (B_KV_GROUPS * HEAD_DIM,) * 6 + (B_HEADS * 3,))
    seg = lambda k: w[:, int(o[k]):int(o[k + 1])]
    gates = seg(12)
    ng = gates.shape[1] // B_KV_GROUPS
    parts = [seg(0), seg(2), seg(5)] + [seg(k) for k in range(6, 12)] + [
        seg(1), _pad_cols(seg(3), LANES), _pad_cols(seg(4), LANES)] + [
        _pad_cols(gates[:, g * ng:(g + 1) * ng], LANES) for g in range(B_KV_GROUPS)]
    return jnp.concatenate(parts, axis=1).astype(BF16)


def _pack_w_in1(w):
    dt = w[:, L1_DT:]
    parts = [w[:, :L1_DT]] + [_pad_cols(dt[:, g * SSD_REP:(g + 1) * SSD_REP], LANES) for g in range(SSD_GROUPS)]
    return jnp.concatenate(parts, axis=1).astype(BF16)


def _per_group(v):
    return jnp.pad(v.reshape(SSD_GROUPS, 1, SSD_REP), ((0, 0), (0, 0), (0, LANES - SSD_REP)))


def _peer_block(x, wq, keys, u, v, g, b):
    stats = _peer_stats(x, wq, keys)
    return _peer_ln(x, stats, u, v, g, b)


def _even_layer(x, bsz, seq, w_in, kv_gain, w_uk, w_uv, cmpk_pos, cmpk_w1, cmpk_w2, cmpv_pos, cmpv_w1, cmpv_w2,
                w_out, ln1_g, ln1_b, peer_wq, peer_keys, peer_u, peer_v, ln2_g, ln2_b):
    t = bsz * seq
    y0 = _matmul(x, _pack_w_in0(w_in)).reshape(bsz, seq, NP0)
    o_a = _dsa(y0, kv_gain, w_uk, w_uv)
    kvc = _nsa_compress(y0, jnp.stack([cmpk_pos, cmpv_pos]), jnp.stack([cmpk_w1, cmpv_w1]),
                        jnp.stack([cmpk_w2, cmpv_w2]))
    o_b = _nsa(y0, kvc)
    x = _out_ln(o_a.reshape(t, -1), o_b.reshape(t, -1), w_out, x, ln1_g, ln1_b)
    return _peer_block(x, peer_wq, peer_keys, peer_u, peer_v, ln2_g, ln2_b)


def _odd_layer(x, bsz, seq, w_in, conv_w, ssd_conv_w, ssd_conv_b, dt_bias, a_log, d_skip, norm_g,
               w_out, ln1_g, ln1_b, peer_wq, peer_keys, peer_u, peer_v, ln2_g, ln2_b):
    t = bsz * seq
    y1 = _matmul(x, _pack_w_in1(w_in), tn=1536).reshape(bsz, seq, NP1)
    y_c = _shortconv(y1, conv_w)
    xbc = _ssdconv(y1, ssd_conv_w, ssd_conv_b)
    y_d = _ssd(y1, xbc, _per_group(dt_bias), _per_group(-jnp.exp(a_log)), d_skip, norm_g)
    x = _out_ln(y_c.reshape(t, -1), y_d.reshape(t, -1), w_out, x, ln1_g, ln1_b)
    return _peer_block(x, peer_wq, peer_keys, peer_u, peer_v, ln2_g, ln2_b)


def kernel(x, l0_w_in, l0_kv_gain, l0_w_uk, l0_w_uv, l0_cmpk_pos, l0_cmpk_w1, l0_cmpk_w2, l0_cmpv_pos, l0_cmpv_w1, l0_cmpv_w2, l0_w_out, l0_ln1_g, l0_ln1_b, l0_peer_wq, l0_peer_keys, l0_peer_u, l0_peer_v, l0_ln2_g, l0_ln2_b, l1_w_in, l1_conv_w, l1_ssd_conv_w, l1_ssd_conv_b, l1_dt_bias, l1_a_log, l1_d_skip, l1_norm_g, l1_w_out, l1_ln1_g, l1_ln1_b, l1_peer_wq, l1_peer_keys, l1_peer_u, l1_peer_v, l1_ln2_g, l1_ln2_b):
    bsz, seq, d = x.shape
    h = x.reshape(bsz * seq, d)
    h = _even_layer(h, bsz, seq, l0_w_in, l0_kv_gain, l0_w_uk, l0_w_uv, l0_cmpk_pos, l0_cmpk_w1, l0_cmpk_w2,
                    l0_cmpv_pos, l0_cmpv_w1, l0_cmpv_w2, l0_w_out, l0_ln1_g, l0_ln1_b,
                    l0_peer_wq, l0_peer_keys, l0_peer_u, l0_peer_v, l0_ln2_g, l0_ln2_b)
    h = _odd_layer(h, bsz, seq, l1_w_in, l1_conv_w, l1_ssd_conv_w, l1_ssd_conv_b, l1_dt_bias, l1_a_log,
                   l1_d_skip, l1_norm_g, l1_w_out, l1_ln1_g, l1_ln1_b,
                   l1_peer_wq, l1_peer_keys, l1_peer_u, l1_peer_v, l1_ln2_g, l1_ln2_b)
    return h.reshape(bsz, seq, d)
```
